```python
import math
import jax, jax.numpy as jnp
from jax import lax
import numpy as np

D_MODEL = 1024
BATCH = 8
SEQ = 2048
DEPTH = 4

D_BRANCH = D_MODEL
S5_GROUP = 16
S5_GROUPS = D_BRANCH // S5_GROUP
S5_STATE = 64
S5_DT_MIN = 1e-3
S5_DT_MAX = 1e-1
POOL_WINDOWS = (2, 4, 8, 16)
POOL_GROUP = D_BRANCH // len(POOL_WINDOWS)
CONV_EXPAND = 2
D_CONV = CONV_EXPAND * D_MODEL
CONV_WIDTH = 3
N_EVEN = (DEPTH + 1) // 2
N_ODD = DEPTH // 2
RMS_EPS = 1e-6

kernel_name = "hybrid_s5_pool_shortconv_trunk"


def rmsnorm(x, g):
    xf = x.astype(jnp.float32)
    return xf * lax.rsqrt(jnp.mean(xf * xf, axis=-1, keepdims=True) + RMS_EPS) * g


def _cmul(ar, ai, br, bi):
    return ar * br - ai * bi, ar * bi + ai * br


def _scan_combine(e1, e2):
    a1r, a1i, b1r, b1i = e1
    a2r, a2i, b2r, b2i = e2
    ar, ai = _cmul(a2r, a2i, a1r, a1i)
    br, bi = _cmul(a2r, a2i, b1r, b1i)
    return ar, ai, br + b2r, bi + b2i


def s5_mixer(u, a_re, a_im, log_dt, b_re, b_im, c_re, c_im, d_skip, w_glu, b_glu):
    f32 = jnp.float32
    bsz, seq, _ = u.shape
    u = u.astype(f32)
    ug = u.reshape(bsz, seq, S5_GROUPS, S5_GROUP)
    dt = jnp.exp(log_dt.astype(f32))[:, None]
    lam_re = a_re.astype(f32)
    lam_im = a_im.astype(f32)
    mag = jnp.exp(lam_re * dt)
    ang = lam_im * dt
    lb_re = mag * jnp.cos(ang)
    lb_im = mag * jnp.sin(ang)
    den = lam_re * lam_re + lam_im * lam_im
    f_re = ((lb_re - 1.0) * lam_re + lb_im * lam_im) / den
    f_im = (lb_im * lam_re - (lb_re - 1.0) * lam_im) / den
    bb_re, bb_im = _cmul(f_re[..., None], f_im[..., None],
                         b_re.astype(f32), b_im.astype(f32))
    bu_re = jnp.einsum('blgh,gph->blgp', ug, bb_re)
    bu_im = jnp.einsum('blgh,gph->blgp', ug, bb_im)
    a_r = jnp.broadcast_to(lb_re, (1, seq) + lb_re.shape)
    a_i = jnp.broadcast_to(lb_im, (1, seq) + lb_im.shape)
    _, _, s_re, s_im = lax.associative_scan(
        _scan_combine, (a_r, a_i, bu_re, bu_im), axis=1)
    y = (jnp.einsum('blgp,ghp->blgh', s_re, c_re.astype(f32))
         - jnp.einsum('blgp,ghp->blgh', s_im, c_im.astype(f32)))
    y = y.reshape(bsz, seq, D_BRANCH) + d_skip * u
    y = jax.nn.gelu(y)
    z = y @ w_glu + b_glu
    val, gate = jnp.split(z, 2, axis=-1)
    return val * jax.nn.sigmoid(gate)


def pool_mixer(u, w_pool, scale):
    bsz, seq, _ = u.shape
    uf = u.astype(jnp.float32)
    cs = jnp.cumsum(uf, axis=1)
    pos = jnp.arange(seq)
    pooled = []
    for gi, w in enumerate(POOL_WINDOWS):
        c = cs[..., gi * POOL_GROUP:(gi + 1) * POOL_GROUP]
        shifted = jnp.pad(c, ((0, 0), (w, 0), (0, 0)))[:, :seq]
        count = jnp.minimum(pos + 1, w).astype(jnp.float32)[:, None]
        pooled.append((c - shifted) / count)
    diff = jnp.concatenate(pooled, axis=-1) - uf
    diff = diff.reshape(bsz, seq, len(POOL_WINDOWS), POOL_GROUP)
    out = jnp.einsum('blgc,gcd->blgd', diff, w_pool).reshape(bsz, seq, D_BRANCH)
    return out * scale


def short_conv_mixer(h, w_in, conv_w, conv_b, w_out):
    z = h @ w_in
    x_in, b_gate, c_gate, g = jnp.split(z, 4, axis=-1)
    v = c_gate * x_in
    conv = lax.conv_general_dilated(
        v, conv_w.astype(v.dtype)[:, None, :], window_strides=(1,),
        padding=[(CONV_WIDTH - 1, 0)], dimension_numbers=('NWC', 'WIO', 'NWC'),
        feature_group_count=D_CONV) + conv_b
    y = b_gate * conv * jax.nn.silu(g)
    return y @ w_out


def setup_inputs(seed: int = 0) -> dict:
    key = jax.random.key(seed)
    ks = jax.random.split(key, 24)
    f32 = jnp.float32
    nrm = lambda k, shape, s: jax.random.normal(k, shape, f32) * s
    x = jax.random.normal(ks[0], (BATCH, SEQ, D_MODEL), f32)
    norm_g = 1.0 + nrm(ks[1], (DEPTH, D_MODEL), 0.02)
    final_g = 1.0 + nrm(ks[2], (D_MODEL,), 0.02)
    ev_w_in = nrm(ks[3], (N_EVEN, D_MODEL, 4 * D_BRANCH), D_MODEL ** -0.5)
    ev_w_out = nrm(ks[4], (N_EVEN, 2 * D_BRANCH, D_MODEL), (2 * D_BRANCH) ** -0.5)
    n_idx = jnp.arange(S5_STATE, dtype=f32)
    s5_a_re = -0.5 + nrm(ks[5], (N_EVEN, S5_GROUPS, S5_STATE), 0.01)
    s5_a_im = math.pi * n_idx + nrm(ks[6], (N_EVEN, S5_GROUPS, S5_STATE), 0.01)
    s5_log_dt = jax.random.uniform(ks[7], (N_EVEN, S5_GROUPS), f32,
                                   math.log(S5_DT_MIN), math.log(S5_DT_MAX))
    bs = (2.0 * S5_GROUP) ** -0.5
    cscale = (2.0 * S5_STATE) ** -0.5
    s5_b_re = nrm(ks[8], (N_EVEN, S5_GROUPS, S5_STATE, S5_GROUP), bs)
    s5_b_im = nrm(ks[9], (N_EVEN, S5_GROUPS, S5_STATE, S5_GROUP), bs)
    s5_c_re = nrm(ks[10], (N_EVEN, S5_GROUPS, S5_GROUP, S5_STATE), cscale)
    s5_c_im = nrm(ks[11], (N_EVEN, S5_GROUPS, S5_GROUP, S5_STATE), cscale)
    s5_d = nrm(ks[12], (N_EVEN, D_BRANCH), 1.0)
    s5_w_glu = nrm(ks[13], (N_EVEN, D_BRANCH, 2 * D_BRANCH), D_BRANCH ** -0.5)
    s5_b_glu = nrm(ks[14], (N_EVEN, 2 * D_BRANCH), 0.01)
    pool_w = nrm(ks[15], (N_EVEN, len(POOL_WINDOWS), POOL_GROUP, POOL_GROUP), POOL_GROUP ** -0.5)
    pool_scale = 1.0 + nrm(ks[16], (N_EVEN, D_BRANCH), 0.02)
    sc_w_in = nrm(ks[17], (N_ODD, D_MODEL, 4 * D_CONV), D_MODEL ** -0.5)
    sc_conv_w = nrm(ks[18], (N_ODD, CONV_WIDTH, D_CONV), CONV_WIDTH ** -0.5)
    sc_conv_b = nrm(ks[19], (N_ODD, D_CONV), 0.01)
    sc_w_out = nrm(ks[20], (N_ODD, D_CONV, D_MODEL), D_CONV ** -0.5)
    return {"x": x, "norm_g": norm_g, "final_g": final_g,
            "ev_w_in": ev_w_in, "ev_w_out": ev_w_out,
            "s5_a_re": s5_a_re, "s5_a_im": s5_a_im, "s5_log_dt": s5_log_dt,
            "s5_b_re": s5_b_re, "s5_b_im": s5_b_im, "s5_c_re": s5_c_re, "s5_c_im": s5_c_im,
            "s5_d": s5_d, "s5_w_glu": s5_w_glu, "s5_b_glu": s5_b_glu,
            "pool_w": pool_w, "pool_scale": pool_scale,
            "sc_w_in": sc_w_in, "sc_conv_w": sc_conv_w, "sc_conv_b": sc_conv_b,
            "sc_w_out": sc_w_out}


def reference(x, norm_g, final_g, ev_w_in, ev_w_out, s5_a_re, s5_a_im, s5_log_dt,
              s5_b_re, s5_b_im, s5_c_re, s5_c_im, s5_d, s5_w_glu, s5_b_glu,
              pool_w, pool_scale, sc_w_in, sc_conv_w, sc_conv_b, sc_w_out):
    out_dtype = x.dtype
    h_res = x.astype(jnp.float32)
    for layer in range(DEPTH):
        h = rmsnorm(h_res, norm_g[layer])
        if layer % 2 == 0:
            i = layer // 2
            z = h @ ev_w_in[i]
            u_a, g_a, u_b, g_b = jnp.split(z, 4, axis=-1)
            y_a = s5_mixer(u_a, s5_a_re[i], s5_a_im[i], s5_log_dt[i], s5_b_re[i], s5_b_im[i],
                           s5_c_re[i], s5_c_im[i], s5_d[i], s5_w_glu[i], s5_b_glu[i])
            y_b = pool_mixer(u_b, pool_w[i], pool_scale[i])
            y = jnp.concatenate([y_a * jax.nn.silu(g_a), y_b * jax.nn.silu(g_b)], axis=-1)
            h_res = h_res + y @ ev_w_out[i]
        else:
            i = layer // 2
            h_res = h_res + short_conv_mixer(h, sc_w_in[i], sc_conv_w[i], sc_conv_b[i], sc_w_out[i])
    return rmsnorm(h_res, final_g).astype(out_dtype)
```

```python
import functools
import math

import jax
import jax.numpy as jnp
from jax import lax
from jax.experimental import pallas as pl
from jax.experimental.pallas import tpu as pltpu

F32 = jnp.float32
BF16 = jnp.bfloat16

RMS_EPS = 1e-6
S5_GROUP = 16
S5_STATE = 64
POOL_WINDOWS = (2, 4, 8, 16)
CONV_WIDTH = 3

SUBLANES = 8
S5_CHUNK = 128
GROUPS_PER_CHUNK = S5_CHUNK // S5_GROUP
STATES_PER_CHUNK = GROUPS_PER_CHUNK * S5_STATE

EVEN_TILE_STEPS = 32
ODD_TILE_STEPS = 64
CONV_CHUNK = 512
VMEM_LIMIT_BYTES = 56 * 1024 * 1024


def _dot(a, b):
    return jnp.dot(a, b, preferred_element_type=F32)


def _rmsnorm(x, g):
    ms = jnp.mean(x * x, axis=-1, keepdims=True)
    return x * lax.rsqrt(ms + RMS_EPS) * g


def _sigmoid(x):
    return 1.0 / (1.0 + jnp.exp(-x))


def _silu(x):
    return x * _sigmoid(x)


def _gelu_tanh(x):
    c = math.sqrt(2.0 / math.pi)
    return 0.5 * x * (1.0 + jnp.tanh(c * (x + 0.044715 * (x * x * x))))


def _s5_prep_kernel(are_ref, aim_ref, logdt_ref, bre_ref, bim_ref,
                    lbre_ref, lbim_ref, bbre_ref, bbim_ref):
    lam_re = are_ref[...]
    lam_im = aim_ref[...]
    dt = jnp.exp(logdt_ref[...])
    mag = jnp.exp(lam_re * dt)
    ang = lam_im * dt
    lb_re = mag * jnp.cos(ang)
    lb_im = mag * jnp.sin(ang)
    den = lam_re * lam_re + lam_im * lam_im
    f_re = ((lb_re - 1.0) * lam_re + lb_im * lam_im) / den
    f_im = (lb_im * lam_re - (lb_re - 1.0) * lam_im) / den
    b_re = bre_ref[...]
    b_im = bim_ref[...]
    lbre_ref[...] = lb_re
    lbim_ref[...] = lb_im
    bbre_ref[...] = f_re * b_re - f_im * b_im
    bbim_ref[...] = f_re * b_im + f_im * b_re


def _s5_prep(a_re, a_im, log_dt, b_re, b_im):
    n, g, p = a_re.shape
    h = b_re.shape[-1]
    full = (n, g, h, p)
    a_re_b = jnp.broadcast_to(a_re[:, :, None, :], full)
    a_im_b = jnp.broadcast_to(a_im[:, :, None, :], full)
    dt_b = jnp.broadcast_to(log_dt[:, :, None, None], full)
    bt_re = jnp.swapaxes(b_re, 2, 3)
    bt_im = jnp.swapaxes(b_im, 2, 3)
    spec = pl.BlockSpec((None, g, h, p), lambda i: (i, 0, 0, 0))
    out = jax.ShapeDtypeStruct(full, F32)
    lb_re, lb_im, bb_re, bb_im = pl.pallas_call(
        _s5_prep_kernel,
        grid=(n,),
        in_specs=[spec] * 5,
        out_specs=[spec] * 4,
        out_shape=[out] * 4,
        name="s5_prep",
    )(a_re_b, a_im_b, dt_b, bt_re, bt_im)
    return lb_re[:, :, 0, :], lb_im[:, :, 0, :], bb_re, bb_im


def _s5_block_matrices(lb_re, lb_im, bbt_re, bbt_im, c_re, c_im):
    g, hh, p = bbt_re.shape
    nc = g // GROUPS_PER_CHUNK
    eye = jnp.eye(GROUPS_PER_CHUNK, dtype=F32)

    def b_blocks(bbt):
        v = bbt.reshape(nc, GROUPS_PER_CHUNK, hh, p)
        m = jnp.einsum('jghp,gk->jghkp', v, eye)
        return m.reshape(nc, S5_CHUNK, STATES_PER_CHUNK)

    def c_blocks(c):
        v = c.reshape(nc, GROUPS_PER_CHUNK, hh, p)
        m = jnp.einsum('jghp,gk->jkpgh', v, eye)
        return m.reshape(nc, STATES_PER_CHUNK, S5_CHUNK)

    b_mat = jnp.concatenate([b_blocks(bbt_re), b_blocks(bbt_im)], axis=2).astype(BF16)
    c_mat = jnp.concatenate([c_blocks(c_re), c_blocks(-c_im)], axis=1).astype(BF16)

    def lam(x):
        v = x.reshape(nc, 1, STATES_PER_CHUNK)
        return jnp.broadcast_to(v, (nc, SUBLANES, STATES_PER_CHUNK))

    return lam(lb_re), lam(lb_im), b_mat, c_mat


def _even_kernel(x_ref, g_ref, win_ref, bmat_ref, lre_ref, lim_ref, cmat_ref,
                 d_ref, wglu_ref, bglu_ref, wpool_ref, pscale_ref, wout_ref,
                 o_ref,
                 st_re, st_im, bu_scr, s_scr, ya_scr, ub_buf):
    i = pl.program_id(0)
    m, d = x_ref.shape
    steps = m // SUBLANES
    sp = STATES_PER_CHUNK
    n_chunks = d // S5_CHUNK
    hist = ub_buf.shape[0] - m

    @pl.when(i == 0)
    def _():
        st_re[...] = jnp.zeros_like(st_re)
        st_im[...] = jnp.zeros_like(st_im)
        ub_buf[0:hist, :] = jnp.zeros((hist, d), F32)

    x = x_ref[...]
    h = _rmsnorm(x, g_ref[...]).astype(BF16)

    u_a = _dot(h, win_ref[:, 0:d])
    for j in range(n_chunks):
        ua_j = u_a[:, j * S5_CHUNK:(j + 1) * S5_CHUNK].astype(BF16)
        bu_scr[...] = _dot(ua_j, bmat_ref[j])
        l_re = lre_ref[j]
        l_im = lim_ref[j]

        def step(t, carry, l_re=l_re, l_im=l_im):
            s_re, s_im = carry
            r0 = pl.multiple_of(t * SUBLANES, SUBLANES)
            bu_re = bu_scr[pl.ds(r0, SUBLANES), 0:sp]
            bu_im = bu_scr[pl.ds(r0, SUBLANES), sp:2 * sp]
            n_re = l_re * s_re - l_im * s_im + bu_re
            n_im = l_re * s_im + l_im * s_re + bu_im
            s_scr[pl.ds(r0, SUBLANES), 0:sp] = n_re
            s_scr[pl.ds(r0, SUBLANES), sp:2 * sp] = n_im
            return n_re, n_im

        s_re, s_im = lax.fori_loop(0, steps, step, (st_re[j], st_im[j]), unroll=8)
        st_re[j] = s_re
        st_im[j] = s_im
        ya_scr[:, j * S5_CHUNK:(j + 1) * S5_CHUNK] = _dot(
            s_scr[...].astype(BF16), cmat_ref[j])

    y_a = _gelu_tanh(ya_scr[...] + d_ref[...] * u_a)
    z_glu = _dot(y_a.astype(BF16), wglu_ref[...]) + bglu_ref[...]
    g_a = _dot(h, win_ref[:, d:2 * d])
    y_a = z_glu[:, 0:d] * _sigmoid(z_glu[:, d:2 * d]) * _silu(g_a)
    acc = _dot(y_a.astype(BF16), wout_ref[0:d, :])

    u_b = _dot(h, win_ref[:, 2 * d:3 * d])
    ub_buf[hist:hist + m, :] = u_b
    pg = d // len(POOL_WINDOWS)
    t_idx = i * steps + (lax.broadcasted_iota(jnp.int32, (m, pg), 0) // SUBLANES)
    parts = []
    for gi, w in enumerate(POOL_WINDOWS):
        c0, c1 = gi * pg, (gi + 1) * pg
        cur = u_b[:, c0:c1]
        win_sum = cur
        for k in range(1, w):
            off = hist - k * SUBLANES
            win_sum = win_sum + ub_buf[off:off + m, c0:c1]
        count = jnp.minimum(t_idx + 1, w).astype(F32)
        diff = win_sum / count - cur
        parts.append(_dot(diff.astype(BF16), wpool_ref[gi]))
    ub_buf[0:hist, :] = ub_buf[m:m + hist, :]
    g_b = _dot(h, win_ref[:, 3 * d:4 * d])
    y_b = jnp.concatenate(parts, axis=1) * pscale_ref[...] * _silu(g_b)
    acc = acc + _dot(y_b.astype(BF16), wout_ref[d:2 * d, :])

    o_ref[...] = x + acc


def _resident(shape):
    nd = len(shape)
    return pl.BlockSpec(shape, lambda i: (0,) * nd, pipeline_mode=pl.Buffered(1))


def _even_layer(xt, norm_g, w_in, lam_re, lam_im, b_mat, c_mat, d_skip, w_glu, b_glu,
                w_pool, pool_scale, w_out):
    rows, d = xt.shape
    m = EVEN_TILE_STEPS * SUBLANES
    hist = max(POOL_WINDOWS) * SUBLANES
    nc = d // S5_CHUNK
    sp = STATES_PER_CHUNK
    row_spec = pl.BlockSpec((m, d), lambda i: (i, 0))
    args = (xt, norm_g.reshape(1, d), w_in.astype(BF16), b_mat, lam_re, lam_im, c_mat,
            d_skip.reshape(1, d), w_glu.astype(BF16), b_glu.reshape(1, 2 * d),
            w_pool.astype(BF16), pool_scale.reshape(1, d), w_out.astype(BF16))
    in_specs = [row_spec] + [_resident(a.shape) for a in args[1:]]
    return pl.pallas_call(
        _even_kernel,
        grid=(rows // m,),
        in_specs=in_specs,
        out_specs=row_spec,
        out_shape=jax.ShapeDtypeStruct((rows, d), F32),
        scratch_shapes=[
            pltpu.VMEM((nc, SUBLANES, sp), F32),
            pltpu.VMEM((nc, SUBLANES, sp), F32),
            pltpu.VMEM((m, 2 * sp), F32),
            pltpu.VMEM((m, 2 * sp), F32),
            pltpu.VMEM((m, d), F32),
            pltpu.VMEM((hist + m, d), F32),
        ],
        compiler_params=pltpu.CompilerParams(
            dimension_semantics=("arbitrary",),
            vmem_limit_bytes=VMEM_LIMIT_BYTES),
        name="even_layer",
    )(*args)


def _odd_kernel(x_ref, g_ref, win_ref, cw_ref, cb_ref, wout_ref, fg_ref, o_ref,
                v_buf, *, final_norm):
    i = pl.program_id(0)
    m, d = x_ref.shape
    dc = wout_ref.shape[0]
    n_chunks = dc // CONV_CHUNK
    hist = v_buf.shape[1] - m

    @pl.when(i == 0)
    def _():
        v_buf[:, 0:hist, :] = jnp.zeros((n_chunks, hist, CONV_CHUNK), F32)

    x = x_ref[...]
    h = _rmsnorm(x, g_ref[...]).astype(BF16)
    acc = jnp.zeros((m, d), F32)
    for c in range(n_chunks):
        c0, c1 = c * CONV_CHUNK, (c + 1) * CONV_CHUNK
        x_in = _dot(h, win_ref[:, c0:c1])
        c_gate = _dot(h, win_ref[:, 2 * dc + c0:2 * dc + c1])
        v = c_gate * x_in
        v_buf[c, hist:hist + m, :] = v
        conv = cb_ref[:, c0:c1] + cw_ref[CONV_WIDTH - 1:CONV_WIDTH, c0:c1] * v
        for k in range(1, CONV_WIDTH):
            off = hist - k * SUBLANES
            tap = CONV_WIDTH - 1 - k
            conv = conv + cw_ref[tap:tap + 1, c0:c1] * v_buf[c, off:off + m, :]
        v_buf[c, 0:hist, :] = v_buf[c, m:m + hist, :]
        b_gate = _dot(h, win_ref[:, dc + c0:dc + c1])
        gate = _dot(h, win_ref[:, 3 * dc + c0:3 * dc + c1])
        y = b_gate * conv * _silu(gate)
        acc = acc + _dot(y.astype(BF16), wout_ref[c0:c1, :])
    out = x + acc
    if final_norm:
        out = _rmsnorm(out, fg_ref[...])
    o_ref[...] = out


def _odd_layer(xt, norm_g, w_in, conv_w, conv_b, w_out, final_g, final_norm):
    rows, d = xt.shape
    dc = w_out.shape[0]
    m = ODD_TILE_STEPS * SUBLANES
    hist = (CONV_WIDTH - 1) * SUBLANES
    row_spec = pl.BlockSpec((m, d), lambda i: (i, 0))
    args = (xt, norm_g.reshape(1, d), w_in.astype(BF16), conv_w, conv_b.reshape(1, dc),
            w_out.astype(BF16), final_g.reshape(1, d))
    in_specs = [row_spec] + [_resident(a.shape) for a in args[1:]]
    return pl.pallas_call(
        functools.partial(_odd_kernel, final_norm=final_norm),
        grid=(rows // m,),
        in_specs=in_specs,
        out_specs=row_spec,
        out_shape=jax.ShapeDtypeStruct((rows, d), F32),
        scratch_shapes=[
            pltpu.VMEM((dc // CONV_CHUNK, hist + m, CONV_CHUNK), F32),
        ],
        compiler_params=pltpu.CompilerParams(
            dimension_semantics=("arbitrary",),
            vmem_limit_bytes=VMEM_LIMIT_BYTES),
        name="odd_layer_final" if final_norm else "odd_layer",
    )(*args)


def kernel(x, norm_g, final_g, ev_w_in, ev_w_out, s5_a_re, s5_a_im, s5_log_dt, s5_b_re, s5_b_im,
           s5_c_re, s5_c_im, s5_d, s5_w_glu, s5_b_glu, pool_w, pool_scale, sc_w_in, sc_conv_w,
           sc_conv_b, sc_w_out):
    bsz, seq, d = x.shape
    assert bsz == SUBLANES, "time-major tiling assumes one timestep per 8-row tile"
    depth = norm_g.shape[0]
    out_dtype = x.dtype

    lb_re, lb_im, bbt_re, bbt_im = _s5_prep(s5_a_re, s5_a_im, s5_log_dt, s5_b_re, s5_b_im)

    xt = jnp.swapaxes(x.astype(F32), 0, 1).reshape(seq * bsz, d)
    for layer in range(depth):
        i = layer // 2
        if layer % 2 == 0:
            lam_re, lam_im, b_mat, c_mat = _s5_block_matrices(
                lb_re[i], lb_im[i], bbt_re[i], bbt_im[i], s5_c_re[i], s5_c_im[i])
            xt = _even_layer(xt, norm_g[layer], ev_w_in[i], lam_re, lam_im, b_mat, c_mat,
                             s5_d[i], s5_w_glu[i], s5_b_glu[i], pool_w[i], pool_scale[i],
                             ev_w_out[i])
        else:
            xt = _odd_layer(xt, norm_g[layer], sc_w_in[i], sc_conv_w[i], sc_conv_b[i],
                            sc_w_out[i], final_g, final_norm=(layer == depth - 1))
    if depth % 2 == 1:
        raise NotImplementedError("final norm is fused into the last odd layer")
    y = jnp.swapaxes(xt.reshape(seq, bsz, d), 0, 1)
    return y.astype(out_dtype)
```

```python
import functools
import math

import jax
import jax.numpy as jnp
from jax import lax
from jax.experimental import pallas as pl
from jax.experimental.pallas import tpu as pltpu

F32 = jnp.float32
BF16 = jnp.bfloat16

RMS_EPS = 1e-6
S5_GROUP = 16
S5_STATE = 64
POOL_WINDOWS = (2, 4, 8, 16)
CONV_WIDTH = 3

SUBLANES = 8
S5_CHUNK = 128
GROUPS_PER_CHUNK = S5_CHUNK // S5_GROUP
STATES_PER_CHUNK = GROUPS_PER_CHUNK * S5_STATE

EVEN_TILE_STEPS = 64
ODD_TILE_STEPS = 64
CONV_CHUNK = 512
VMEM_LIMIT_BYTES = 56 * 1024 * 1024


def _dot(a, b):
    return jnp.dot(a, b, preferred_element_type=F32)


def _rmsnorm(x, g):
    ms = jnp.mean(x * x, axis=-1, keepdims=True)
    return x * lax.rsqrt(ms + RMS_EPS) * g


def _sigmoid(x):
    return 1.0 / (1.0 + jnp.exp(-x))


def _silu(x):
    return x * _sigmoid(x)


def _gelu_tanh(x):
    c = math.sqrt(2.0 / math.pi)
    return 0.5 * x * (1.0 + jnp.tanh(c * (x + 0.044715 * (x * x * x))))


def _s5_prep_kernel(are_ref, aim_ref, logdt_ref, bre_ref, bim_ref,
                    lbre_ref, lbim_ref, bbre_ref, bbim_ref):
    lam_re = are_ref[...]
    lam_im = aim_ref[...]
    dt = jnp.exp(logdt_ref[...])
    mag = jnp.exp(lam_re * dt)
    ang = lam_im * dt
    lb_re = mag * jnp.cos(ang)
    lb_im = mag * jnp.sin(ang)
    den = lam_re * lam_re + lam_im * lam_im
    f_re = ((lb_re - 1.0) * lam_re + lb_im * lam_im) / den
    f_im = (lb_im * lam_re - (lb_re - 1.0) * lam_im) / den
    b_re = bre_ref[...]
    b_im = bim_ref[...]
    lbre_ref[...] = lb_re
    lbim_ref[...] = lb_im
    bbre_ref[...] = f_re * b_re - f_im * b_im
    bbim_ref[...] = f_re * b_im + f_im * b_re


def _s5_prep(a_re, a_im, log_dt, b_re, b_im):
    n, g, p = a_re.shape
    h = b_re.shape[-1]
    full = (n, g, h, p)
    a_re_b = jnp.broadcast_to(a_re[:, :, None, :], full)
    a_im_b = jnp.broadcast_to(a_im[:, :, None, :], full)
    dt_b = jnp.broadcast_to(log_dt[:, :, None, None], full)
    bt_re = jnp.swapaxes(b_re, 2, 3)
    bt_im = jnp.swapaxes(b_im, 2, 3)
    spec = pl.BlockSpec((None, g, h, p), lambda i: (i, 0, 0, 0))
    out = jax.ShapeDtypeStruct(full, F32)
    lb_re, lb_im, bb_re, bb_im = pl.pallas_call(
        _s5_prep_kernel,
        grid=(n,),
        in_specs=[spec] * 5,
        out_specs=[spec] * 4,
        out_shape=[out] * 4,
        name="s5_prep",
    )(a_re_b, a_im_b, dt_b, bt_re, bt_im)
    return lb_re[:, :, 0, :], lb_im[:, :, 0, :], bb_re, bb_im


def _s5_block_matrices(lb_re, lb_im, bbt_re, bbt_im, c_re, c_im):
    g, hh, p = bbt_re.shape
    nc = g // GROUPS_PER_CHUNK
    eye = jnp.eye(GROUPS_PER_CHUNK, dtype=F32)

    def b_blocks(bbt):
        v = bbt.reshape(nc, GROUPS_PER_CHUNK, hh, p)
        m = jnp.einsum('jghp,gk->jghkp', v, eye)
        return m.reshape(nc, S5_CHUNK, STATES_PER_CHUNK)

    def c_blocks(c):
        v = c.reshape(nc, GROUPS_PER_CHUNK, hh, p)
        m = jnp.einsum('jghp,gk->jkpgh', v, eye)
        return m.reshape(nc, STATES_PER_CHUNK, S5_CHUNK)

    b_mat = jnp.concatenate([b_blocks(bbt_re), b_blocks(bbt_im)], axis=2).astype(BF16)
    c_mat = jnp.concatenate([c_blocks(c_re), c_blocks(-c_im)], axis=1).astype(BF16)

    def lam(x):
        v = x.reshape(nc, 1, STATES_PER_CHUNK)
        return jnp.broadcast_to(v, (nc, SUBLANES, STATES_PER_CHUNK))

    return lam(lb_re), lam(lb_im), b_mat, c_mat


def _even_kernel(x_ref, g_ref, win_ref, bmat_ref, lre_ref, lim_ref, cmat_ref,
                 d_ref, wglu_ref, bglu_ref, wpool_ref, pscale_ref, wout_ref,
                 o_ref,
                 st_re, st_im, bu_scr, s_scr, ya_scr, ub_buf):
    i = pl.program_id(0)
    m, d = x_ref.shape
    steps = m // SUBLANES
    sp = STATES_PER_CHUNK
    n_chunks = d // S5_CHUNK
    hist = ub_buf.shape[0] - m

    @pl.when(i == 0)
    def _():
        st_re[...] = jnp.zeros_like(st_re)
        st_im[...] = jnp.zeros_like(st_im)
        ub_buf[0:hist, :] = jnp.zeros((hist, d), F32)

    x = x_ref[...]
    h = _rmsnorm(x, g_ref[...]).astype(BF16)

    u_a = _dot(h, win_ref[:, 0:d])
    for j in range(n_chunks):
        slot = j % 2
        ua_j = u_a[:, j * S5_CHUNK:(j + 1) * S5_CHUNK].astype(BF16)
        bu_scr[slot] = _dot(ua_j, bmat_ref[j])
        l_re = lre_ref[j]
        l_im = lim_ref[j]
        s_re = st_re[j]
        s_im = st_im[j]
        for t in range(steps):
            r0, r1 = t * SUBLANES, (t + 1) * SUBLANES
            bu_re = bu_scr[slot, r0:r1, 0:sp]
            bu_im = bu_scr[slot, r0:r1, sp:2 * sp]
            n_re = l_re * s_re - l_im * s_im + bu_re
            n_im = l_re * s_im + l_im * s_re + bu_im
            s_scr[slot, r0:r1, 0:sp] = n_re
            s_scr[slot, r0:r1, sp:2 * sp] = n_im
            s_re, s_im = n_re, n_im
        st_re[j] = s_re
        st_im[j] = s_im
        ya_scr[:, j * S5_CHUNK:(j + 1) * S5_CHUNK] = _dot(
            s_scr[slot].astype(BF16), cmat_ref[j])

    y_a = _gelu_tanh(ya_scr[...] + d_ref[...] * u_a)
    z_glu = _dot(y_a.astype(BF16), wglu_ref[...]) + bglu_ref[...]
    g_a = _dot(h, win_ref[:, d:2 * d])
    y_a = z_glu[:, 0:d] * _sigmoid(z_glu[:, d:2 * d]) * _silu(g_a)
    acc = _dot(y_a.astype(BF16), wout_ref[0:d, :])

    u_b = _dot(h, win_ref[:, 2 * d:3 * d])
    ub_buf[hist:hist + m, :] = u_b
    pg = d // len(POOL_WINDOWS)
    t_idx = i * steps + (lax.broadcasted_iota(jnp.int32, (m, pg), 0) // SUBLANES)
    parts = []
    for gi, w in enumerate(POOL_WINDOWS):
        c0, c1 = gi * pg, (gi + 1) * pg
        cur = u_b[:, c0:c1]
        win_sum = cur
        for k in range(1, w):
            off = hist - k * SUBLANES
            win_sum = win_sum + ub_buf[off:off + m, c0:c1]
        count = jnp.minimum(t_idx + 1, w).astype(F32)
        diff = win_sum / count - cur
        parts.append(_dot(diff.astype(BF16), wpool_ref[gi]))
    ub_buf[0:hist, :] = ub_buf[m:m + hist, :]
    g_b = _dot(h, win_ref[:, 3 * d:4 * d])
    y_b = jnp.concatenate(parts, axis=1) * pscale_ref[...] * _silu(g_b)
    acc = acc + _dot(y_b.astype(BF16), wout_ref[d:2 * d, :])

    o_ref[...] = x + acc


def _resident(shape):
    nd = len(shape)
    return pl.BlockSpec(shape, lambda i: (0,) * nd, pipeline_mode=pl.Buffered(1))


def _even_layer(xt, norm_g, w_in, lam_re, lam_im, b_mat, c_mat, d_skip, w_glu, b_glu,
                w_pool, pool_scale, w_out):
    rows, d = xt.shape
    m = EVEN_TILE_STEPS * SUBLANES
    hist = max(POOL_WINDOWS) * SUBLANES
    nc = d // S5_CHUNK
    sp = STATES_PER_CHUNK
    row_spec = pl.BlockSpec((m, d), lambda i: (i, 0))
    args = (xt, norm_g.reshape(1, d), w_in.astype(BF16), b_mat, lam_re, lam_im, c_mat,
            d_skip.reshape(1, d), w_glu.astype(BF16), b_glu.reshape(1, 2 * d),
            w_pool.astype(BF16), pool_scale.reshape(1, d), w_out.astype(BF16))
    in_specs = [row_spec] + [_resident(a.shape) for a in args[1:]]
    return pl.pallas_call(
        _even_kernel,
        grid=(rows // m,),
        in_specs=in_specs,
        out_specs=row_spec,
        out_shape=jax.ShapeDtypeStruct((rows, d), F32),
        scratch_shapes=[
            pltpu.VMEM((nc, SUBLANES, sp), F32),
            pltpu.VMEM((nc, SUBLANES, sp), F32),
            pltpu.VMEM((2, m, 2 * sp), F32),
            pltpu.VMEM((2, m, 2 * sp), F32),
            pltpu.VMEM((m, d), F32),
            pltpu.VMEM((hist + m, d), F32),
        ],
        compiler_params=pltpu.CompilerParams(
            dimension_semantics=("arbitrary",),
            vmem_limit_bytes=VMEM_LIMIT_BYTES),
        name="even_layer",
    )(*args)


def _odd_kernel(x_ref, g_ref, win_ref, cw_ref, cb_ref, wout_ref, fg_ref, o_ref,
                v_buf, *, final_norm):
    i = pl.program_id(0)
    m, d = x_ref.shape
    dc = wout_ref.shape[0]
    n_chunks = dc // CONV_CHUNK
    hist = v_buf.shape[1] - m

    @pl.when(i == 0)
    def _():
        v_buf[:, 0:hist, :] = jnp.zeros((n_chunks, hist, CONV_CHUNK), F32)

    x = x_ref[...]
    h = _rmsnorm(x, g_ref[...]).astype(BF16)
    acc = jnp.zeros((m, d), F32)
    for c in range(n_chunks):
        c0, c1 = c * CONV_CHUNK, (c + 1) * CONV_CHUNK
        x_in = _dot(h, win_ref[:, c0:c1])
        c_gate = _dot(h, win_ref[:, 2 * dc + c0:2 * dc + c1])
        v = c_gate * x_in
        v_buf[c, hist:hist + m, :] = v
        conv = cb_ref[:, c0:c1] + cw_ref[CONV_WIDTH - 1:CONV_WIDTH, c0:c1] * v
        for k in range(1, CONV_WIDTH):
            off = hist - k * SUBLANES
            tap = CONV_WIDTH - 1 - k
            conv = conv + cw_ref[tap:tap + 1, c0:c1] * v_buf[c, off:off + m, :]
        v_buf[c, 0:hist, :] = v_buf[c, m:m + hist, :]
        b_gate = _dot(h, win_ref[:, dc + c0:dc + c1])
        gate = _dot(h, win_ref[:, 3 * dc + c0:3 * dc + c1])
        y = b_gate * conv * _silu(gate)
        acc = acc + _dot(y.astype(BF16), wout_ref[c0:c1, :])
    out = x + acc
    if final_norm:
        out = _rmsnorm(out, fg_ref[...])
    o_ref[...] = out


def _odd_layer(xt, norm_g, w_in, conv_w, conv_b, w_out, final_g, final_norm):
    rows, d = xt.shape
    dc = w_out.shape[0]
    m = ODD_TILE_STEPS * SUBLANES
    hist = (CONV_WIDTH - 1) * SUBLANES
    row_spec = pl.BlockSpec((m, d), lambda i: (i, 0))
    args = (xt, norm_g.reshape(1, d), w_in.astype(BF16), conv_w, conv_b.reshape(1, dc),
            w_out.astype(BF16), final_g.reshape(1, d))
    in_specs = [row_spec] + [_resident(a.shape) for a in args[1:]]
    return pl.pallas_call(
        functools.partial(_odd_kernel, final_norm=final_norm),
        grid=(rows // m,),
        in_specs=in_specs,
        out_specs=row_spec,
        out_shape=jax.ShapeDtypeStruct((rows, d), F32),
        scratch_shapes=[
            pltpu.VMEM((dc // CONV_CHUNK, hist + m, CONV_CHUNK), F32),
        ],
        compiler_params=pltpu.CompilerParams(
            dimension_semantics=("arbitrary",),
            vmem_limit_bytes=VMEM_LIMIT_BYTES),
        name="odd_layer_final" if final_norm else "odd_layer",
    )(*args)


def kernel(x, norm_g, final_g, ev_w_in, ev_w_out, s5_a_re, s5_a_im, s5_log_dt, s5_b_re, s5_b_im,
           s5_c_re, s5_c_im, s5_d, s5_w_glu, s5_b_glu, pool_w, pool_scale, sc_w_in, sc_conv_w,
           sc_conv_b, sc_w_out):
    bsz, seq, d = x.shape
    assert bsz == SUBLANES, "time-major tiling assumes one timestep per 8-row tile"
    depth = norm_g.shape[0]
    assert depth % 2 == 0, "the final norm is fused into the last (odd) layer"
    out_dtype = x.dtype

    lb_re, lb_im, bbt_re, bbt_im = _s5_prep(s5_a_re, s5_a_im, s5_log_dt, s5_b_re, s5_b_im)

    xt = jnp.swapaxes(x.astype(F32), 0, 1).reshape(seq * bsz, d)
    for layer in range(depth):
        i = layer // 2
        if layer % 2 == 0:
            lam_re, lam_im, b_mat, c_mat = _s5_block_matrices(
                lb_re[i], lb_im[i], bbt_re[i], bbt_im[i], s5_c_re[i], s5_c_im[i])
            xt = _even_layer(xt, norm_g[layer], ev_w_in[i], lam_re, lam_im, b_mat, c_mat,
                             s5_d[i], s5_w_glu[i], s5_b_glu[i], pool_w[i], pool_scale[i],
                             ev_w_out[i])
        else:
            xt = _odd_layer(xt, norm_g[layer], sc_w_in[i], sc_conv_w[i], sc_conv_b[i],
                            sc_w_out[i], final_g, final_norm=(layer == depth - 1))
    y =jnp.swapaxes(xt.reshape(seq, bsz, d), 0, 1)
    return y.astype(out_dtype)
```

```python
import functools
import math

import jax
import jax.numpy as jnp
from jax import lax
from jax.experimental import pallas as pl
from jax.experimental.pallas import tpu as pltpu

F32 = jnp.float32
BF16 = jnp.bfloat16

RMS_EPS = 1e-6
S5_GROUP = 16
S5_STATE = 64
POOL_WINDOWS = (2, 4, 8, 16)
CONV_WIDTH = 3

SUBLANES = 8
S5_CHUNK = 128
GROUPS_PER_CHUNK = S5_CHUNK // S5_GROUP
STATES_PER_CHUNK = GROUPS_PER_CHUNK * S5_STATE
PAIR = 2

EVEN_TILE_STEPS = 64
ODD_TILE_STEPS = 64
CONV_CHUNK = 512
VMEM_LIMIT_BYTES = 56 * 1024 * 1024


def _dot(a, b):
    return jnp.dot(a, b, preferred_element_type=F32)


def _rmsnorm(x, g):
    ms = jnp.mean(x * x, axis=-1, keepdims=True)
    return x * lax.rsqrt(ms + RMS_EPS) * g


def _sigmoid(x):
    return 1.0 / (1.0 + jnp.exp(-x))


def _silu(x):
    return x * _sigmoid(x)


def _gelu_tanh(x):
    c = math.sqrt(2.0 / math.pi)
    return 0.5 * x * (1.0 + jnp.tanh(c * (x + 0.044715 * (x * x * x))))


def _load_rows(x_ref):
    if len(x_ref.shape) == 2:
        return x_ref[...]
    b, steps, d = x_ref.shape
    return jnp.swapaxes(x_ref[...], 0, 1).reshape(steps * b, d)


def _store_rows(o_ref, rows):
    if len(o_ref.shape) == 2:
        o_ref[...] = rows.astype(o_ref.dtype)
    else:
        b, steps, d = o_ref.shape
        o_ref[...] = jnp.swapaxes(rows.reshape(steps, b, d), 0, 1).astype(o_ref.dtype)


def _row_spec(m, d, batch_major):
    if batch_major:
        return pl.BlockSpec((SUBLANES, m // SUBLANES, d), lambda i: (0, i, 0))
    return pl.BlockSpec((m, d), lambda i: (i, 0))


def _layer_spec(stacked, layer):
    nd = stacked.ndim - 1
    return pl.BlockSpec((None,) + stacked.shape[1:], lambda i: (layer,) + (0,) * nd,
                        pipeline_mode=pl.Buffered(1))


def _cmul(ar, ai, br, bi):
    return ar * br - ai * bi, ar * bi + ai * br


def _contract_states(a, b):
    return lax.dot_general(a, b, (((2,), (2,)), ((0,), (0,))),
                           precision=lax.Precision.HIGHEST, preferred_element_type=F32)


def _s5_prep_kernel(are_ref, aim_ref, logdt_ref, bre_ref, bim_ref, cre_ref, cim_ref,
                    lb2re_ref, lb2im_ref, bbre_ref, bbim_ref, lbbre_ref, lbbim_ref,
                    clre_ref, clim_ref, cl2re_ref, cl2im_ref, kt0_ref, kt1_ref):
    lam_re = are_ref[...]
    lam_im = aim_ref[...]
    dt = jnp.exp(logdt_ref[...])
    mag = jnp.exp(lam_re * dt)
    ang = lam_im * dt
    lb_re = mag * jnp.cos(ang)
    lb_im = mag * jnp.sin(ang)
    den = lam_re * lam_re + lam_im * lam_im
    f_re = ((lb_re - 1.0) * lam_re + lb_im * lam_im) / den
    f_im = (lb_im * lam_re - (lb_re - 1.0) * lam_im) / den
    bb_re, bb_im = _cmul(f_re, f_im, bre_ref[...], bim_ref[...])
    lb2_re, lb2_im = _cmul(lb_re, lb_im, lb_re, lb_im)
    lbb_re, lbb_im = _cmul(lb_re, lb_im, bb_re, bb_im)
    c_re = cre_ref[...]
    c_im = cim_ref[...]
    cl_re, cl_im = _cmul(c_re, c_im, lb_re, lb_im)
    cl2_re, cl2_im = _cmul(c_re, c_im, lb2_re, lb2_im)
    lb2re_ref[...] = lb2_re
    lb2im_ref[...] = lb2_im
    bbre_ref[...] = bb_re
    bbim_ref[...] = bb_im
    lbbre_ref[...] = lbb_re
    lbbim_ref[...] = lbb_im
    clre_ref[...] = cl_re
    clim_ref[...] = cl_im
    cl2re_ref[...] = cl2_re
    cl2im_ref[...] = cl2_im
    kt0_ref[...] = _contract_states(bb_re, c_re) - _contract_states(bb_im, c_im)
    kt1_ref[...] = _contract_states(bb_re, cl_re) - _contract_states(bb_im, cl_im)


def _s5_prep(a_re, a_im, log_dt, b_re, b_im, c_re, c_im):
    n, g, p = a_re.shape
    h = b_re.shape[-1]
    full = (n, g, h, p)
    a_re_b = jnp.broadcast_to(a_re[:, :, None, :], full)
    a_im_b = jnp.broadcast_to(a_im[:, :, None, :], full)
    dt_b = jnp.broadcast_to(log_dt[:, :, None, None], full)
    bt_re = jnp.swapaxes(b_re, 2, 3)
    bt_im = jnp.swapaxes(b_im, 2, 3)
    spec = pl.BlockSpec((None, g, h, p), lambda i: (i, 0, 0, 0))
    kspec = pl.BlockSpec((None, g, h, h), lambda i: (i, 0, 0, 0))
    out = jax.ShapeDtypeStruct(full, F32)
    kout = jax.ShapeDtypeStruct((n, g, h, h), F32)
    names = ("lb2_re", "lb2_im", "bb_re", "bb_im", "lbb_re", "lbb_im",
             "cl_re", "cl_im", "cl2_re", "cl2_im", "kt0", "kt1")
    outs = pl.pallas_call(
        _s5_prep_kernel,
        grid=(n,),
        in_specs=[spec] * 7,
        out_specs=[spec] * 10 + [kspec] * 2,
        out_shape=[out] * 10 + [kout] * 2,
        name="s5_prep",
    )(a_re_b, a_im_b, dt_b, bt_re, bt_im, c_re, c_im)
    return dict(zip(names, outs))


def _s5_block_matrices(q):
    n, g, hh, p = q["bb_re"].shape
    nc = g // GROUPS_PER_CHUNK
    eye = jnp.eye(GROUPS_PER_CHUNK, dtype=F32)

    def bd_in(v):
        v = v.reshape(n, nc, GROUPS_PER_CHUNK, hh, p)
        return jnp.einsum('njghp,gk->njghkp', v, eye).reshape(n, nc, S5_CHUNK, STATES_PER_CHUNK)

    def bd_out(v):
        v = v.reshape(n, nc, GROUPS_PER_CHUNK, hh, p)
        return jnp.einsum('njghp,gk->njkpgh', v, eye).reshape(n, nc, STATES_PER_CHUNK, S5_CHUNK)

    def bd_k(v):
        v = v.reshape(n, nc, GROUPS_PER_CHUNK, hh, hh)
        return jnp.einsum('njghq,gk->njghkq', v, eye).reshape(n, nc, S5_CHUNK, S5_CHUNK)

    cat = jnp.concatenate
    b_mat = cat([cat([bd_in(q["lbb_re"]), bd_in(q["lbb_im"])], axis=3),
                 cat([bd_in(q["bb_re"]), bd_in(q["bb_im"])], axis=3)], axis=2)
    c_mat = cat([cat([bd_out(q["cl_re"]), bd_out(q["cl2_re"])], axis=3),
                 cat([bd_out(-q["cl_im"]), bd_out(-q["cl2_im"])], axis=3)], axis=2)
    k0 = bd_k(q["kt0"])
    k_mat = cat([cat([k0, bd_k(q["kt1"])], axis=3),
                 cat([jnp.zeros_like(k0), k0], axis=3)], axis=2)

    def lam(x):
        v = x[:, :, 0, :].reshape(n, nc, 1, STATES_PER_CHUNK)
        return jnp.broadcast_to(v, (n, nc, SUBLANES, STATES_PER_CHUNK))

    return (lam(q["lb2_re"]), lam(q["lb2_im"]),
            b_mat.astype(BF16), c_mat.astype(BF16), k_mat.astype(BF16))


def _even_kernel(x_ref, g_ref, win_ref, bmat_ref, lre_ref, lim_ref, cmat_ref, kmat_ref,
                 d_ref, wglu_ref, bglu_ref, wpool_ref, pscale_ref, wout_ref,
                 o_ref,
                 st_re, st_im, ua_scr, l_scr, e_scr, ya_scr, ub_buf):
    i = pl.program_id(0)
    pairs, prow, d = ya_scr.shape
    m = pairs * prow
    mp = pairs * SUBLANES
    steps = m // SUBLANES
    sp = STATES_PER_CHUNK
    n_chunks = d // S5_CHUNK
    hist = ub_buf.shape[0] - m

    @pl.when(i == 0)
    def _():
        st_re[...] = jnp.zeros_like(st_re)
        st_im[...] = jnp.zeros_like(st_im)
        ub_buf[0:hist, :] = jnp.zeros((hist, d), F32)

    x = _load_rows(x_ref)
    h = _rmsnorm(x, g_ref[...]).astype(BF16)

    u_a = _dot(h, win_ref[:, 0:d])
    ua_scr[...] = u_a.reshape(pairs, prow, d)
    for j in range(n_chunks):
        slot = j % 2
        c0, c1 = j * S5_CHUNK, (j + 1) * S5_CHUNK
        u_first = ua_scr[:, 0:SUBLANES, c0:c1].reshape(mp, S5_CHUNK)
        u_second = ua_scr[:, SUBLANES:prow, c0:c1].reshape(mp, S5_CHUNK)
        u_pair = jnp.concatenate([u_first, u_second], axis=1).astype(BF16)
        l_scr[slot] = _dot(u_pair, bmat_ref[j])
        l_re = lre_ref[j]
        l_im = lim_ref[j]
        s_re = st_re[j]
        s_im = st_im[j]
        for c in range(pairs):
            r0, r1 = c * SUBLANES, (c + 1) * SUBLANES
            e_scr[slot, r0:r1, 0:sp] = s_re
            e_scr[slot, r0:r1, sp:2 * sp] = s_im
            n_re = l_re * s_re - l_im * s_im + l_scr[slot, r0:r1, 0:sp]
            n_im = l_re * s_im + l_im * s_re + l_scr[slot, r0:r1, sp:2 * sp]
            s_re, s_im = n_re, n_im
        st_re[j] = s_re
        st_im[j] = s_im
        y_pair = _dot(e_scr[slot].astype(BF16), cmat_ref[j]) + _dot(u_pair, kmat_ref[j])
        ya_scr[:, 0:SUBLANES, c0:c1] = y_pair[:, 0:S5_CHUNK].reshape(pairs, SUBLANES, S5_CHUNK)
        ya_scr[:, SUBLANES:prow, c0:c1] = y_pair[:, S5_CHUNK:2 * S5_CHUNK].reshape(
            pairs, SUBLANES, S5_CHUNK)

    y_a = _gelu_tanh(ya_scr[...].reshape(m, d) + d_ref[...] * u_a)
    z_glu = _dot(y_a.astype(BF16), wglu_ref[...]) + bglu_ref[...]
    g_a = _dot(h, win_ref[:, d:2 * d])
    y_a = z_glu[:, 0:d] * _sigmoid(z_glu[:, d:2 * d]) * _silu(g_a)
    acc = _dot(y_a.astype(BF16), wout_ref[0:d, :])

    u_b = _dot(h, win_ref[:, 2 * d:3 * d])
    ub_buf[hist:hist + m, :] = u_b
    pg = d // len(POOL_WINDOWS)
    t_idx = i * steps + (lax.broadcasted_iota(jnp.int32, (m, pg), 0) // SUBLANES)
    parts = []
    for gi, w in enumerate(POOL_WINDOWS):
        c0, c1 = gi * pg, (gi + 1) * pg
        cur = u_b[:, c0:c1]
        win_sum = cur
        for k in range(1, w):
            off = hist - k * SUBLANES
            win_sum = win_sum + ub_buf[off:off + m, c0:c1]
        count = jnp.minimum(t_idx + 1, w).astype(F32)
        diff = win_sum / count - cur
        parts.append(_dot(diff.astype(BF16), wpool_ref[gi]))
    ub_buf[0:hist, :] = ub_buf[m:m + hist, :]
    g_b = _dot(h, win_ref[:, 3 * d:4 * d])
    y_b = jnp.concatenate(parts, axis=1) * pscale_ref[...] * _silu(g_b)
    acc = acc + _dot(y_b.astype(BF16), wout_ref[d:2 * d, :])

    _store_rows(o_ref, x + acc)


def _even_layer(x_in, layer, idx, p, batch_major_in):
    d = x_in.shape[-1]
    rows = x_in.size // d
    assert EVEN_TILE_STEPS % PAIR == 0
    m = EVEN_TILE_STEPS * SUBLANES
    pairs = EVEN_TILE_STEPS // PAIR
    mp = pairs * SUBLANES
    hist = max(POOL_WINDOWS) * SUBLANES
    nc = d // S5_CHUNK
    sp = STATES_PER_CHUNK
    stacked = (p["norm_g"], p["ev_w_in"], p["b_mat"], p["lam_re"], p["lam_im"], p["c_mat"],
               p["k_mat"], p["s5_d"], p["s5_w_glu"], p["s5_b_glu"], p["pool_w"],
               p["pool_scale"], p["ev_w_out"])
    idxs = (layer,) + (idx,) * (len(stacked) - 1)
    in_specs = [_row_spec(m, d, batch_major_in)]
    in_specs += [_layer_spec(a, k) for a, k in zip(stacked, idxs)]
    return pl.pallas_call(
        _even_kernel,
        grid=(rows // m,),
        in_specs=in_specs,
        out_specs=_row_spec(m, d, False),
        out_shape=jax.ShapeDtypeStruct((rows, d), F32),
        scratch_shapes=[
            pltpu.VMEM((nc, SUBLANES, sp), F32),
            pltpu.VMEM((nc, SUBLANES, sp), F32),
            pltpu.VMEM((pairs, PAIR * SUBLANES, d), F32),
            pltpu.VMEM((2, mp, 2 * sp), F32),
            pltpu.VMEM((2, mp, 2 * sp), F32),
            pltpu.VMEM((pairs, PAIR * SUBLANES, d), F32),
            pltpu.VMEM((hist + m, d), F32),
        ],
        compiler_params=pltpu.CompilerParams(
            dimension_semantics=("arbitrary",),
            vmem_limit_bytes=VMEM_LIMIT_BYTES),
        name="even_layer",
    )(x_in, *stacked)


def _odd_kernel(x_ref, g_ref, win_ref, cw_ref, cb_ref, wout_ref, fg_ref, o_ref,
                v_buf, *, final_norm):
    i = pl.program_id(0)
    dc, d = wout_ref.shape
    n_chunks = dc // CONV_CHUNK
    hist = (CONV_WIDTH - 1) * SUBLANES
    m = v_buf.shape[1] - hist

    @pl.when(i == 0)
    def _():
        v_buf[:, 0:hist, :] = jnp.zeros((n_chunks, hist, CONV_CHUNK), F32)

    x = _load_rows(x_ref)
    h = _rmsnorm(x, g_ref[...]).astype(BF16)
    acc = jnp.zeros((m, d), F32)
    for c in range(n_chunks):
        c0, c1 = c * CONV_CHUNK, (c + 1) * CONV_CHUNK
        x_in = _dot(h, win_ref[:, c0:c1])
        c_gate = _dot(h, win_ref[:, 2 * dc + c0:2 * dc + c1])
        v = c_gate * x_in
        v_buf[c, hist:hist + m, :] = v
        conv = cb_ref[:, c0:c1] + cw_ref[CONV_WIDTH - 1:CONV_WIDTH, c0:c1] * v
        for k in range(1, CONV_WIDTH):
            off = hist - k * SUBLANES
            tap = CONV_WIDTH - 1 - k
            conv = conv + cw_ref[tap:tap + 1, c0:c1] * v_buf[c, off:off + m, :]
        v_buf[c, 0:hist, :] = v_buf[c, m:m + hist, :]
        b_gate = _dot(h, win_ref[:, dc + c0:dc + c1])
        gate = _dot(h, win_ref[:, 3 * dc + c0:3 * dc + c1])
        y = b_gate * conv * _silu(gate)
        acc = acc + _dot(y.astype(BF16), wout_ref[c0:c1, :])
    out = x + acc
    if final_norm:
        out = _rmsnorm(out, fg_ref[...])
    _store_rows(o_ref, out)


def _odd_layer(xt, layer, idx, p, final_norm, out_dtype):
    rows, d = xt.shape
    dc = p["sc_w_out"].shape[1]
    m = ODD_TILE_STEPS * SUBLANES
    hist = (CONV_WIDTH - 1) * SUBLANES
    stacked = (p["norm_g"], p["sc_w_in"], p["sc_conv_w"], p["sc_conv_b"], p["sc_w_out"],
               p["final_g"])
    idxs = (layer, idx, idx, idx, idx, 0)
    in_specs = [_row_spec(m, d, False)]
    in_specs += [_layer_spec(a, k) for a, k in zip(stacked, idxs)]
    if final_norm:
        out_shape = jax.ShapeDtypeStruct((SUBLANES, rows // SUBLANES, d), out_dtype)
    else:
        out_shape = jax.ShapeDtypeStruct((rows, d), F32)
    return pl.pallas_call(
        functools.partial(_odd_kernel, final_norm=final_norm),
        grid=(rows // m,),
        in_specs=in_specs,
        out_specs=_row_spec(m, d, final_norm),
        out_shape=out_shape,
        scratch_shapes=[
            pltpu.VMEM((dc // CONV_CHUNK, hist + m, CONV_CHUNK), F32),
        ],
        compiler_params=pltpu.CompilerParams(
            dimension_semantics=("arbitrary",),
            vmem_limit_bytes=VMEM_LIMIT_BYTES),
        name="odd_layer_final" if final_norm else "odd_layer",
    )(xt, *stacked)


def kernel(x, norm_g, final_g, ev_w_in, ev_w_out, s5_a_re, s5_a_im, s5_log_dt, s5_b_re, s5_b_im,
           s5_c_re, s5_c_im, s5_d, s5_w_glu, s5_b_glu, pool_w, pool_scale, sc_w_in, sc_conv_w,
           sc_conv_b, sc_w_out):
    bsz, seq, d = x.shape
    assert bsz == SUBLANES, "time-major tiling assumes one timestep per 8-row tile"
    depth = norm_g.shape[0]
    assert depth % 2 == 0, "the final norm is fused into the last (odd) layer"

    folded = _s5_prep(s5_a_re, s5_a_im, s5_log_dt, s5_b_re, s5_b_im, s5_c_re, s5_c_im)
    lam_re, lam_im, b_mat, c_mat, k_mat = _s5_block_matrices(folded)
    p = {
        "norm_g": norm_g[:, None, :], "final_g": final_g[None, None, :],
        "ev_w_in": ev_w_in.astype(BF16), "ev_w_out": ev_w_out.astype(BF16),
        "lam_re": lam_re, "lam_im": lam_im, "b_mat": b_mat, "c_mat": c_mat, "k_mat": k_mat,
        "s5_d": s5_d[:, None, :], "s5_w_glu": s5_w_glu.astype(BF16),
        "s5_b_glu": s5_b_glu[:, None, :], "pool_w": pool_w.astype(BF16),
        "pool_scale": pool_scale[:, None, :],
        "sc_w_in": sc_w_in.astype(BF16), "sc_conv_w": sc_conv_w,
        "sc_conv_b": sc_conv_b[:, None, :], "sc_w_out": sc_w_out.astype(BF16),
    }

    act = x.astype(F32)
    for layer in range(depth):
        idx = layer // 2
        if layer % 2 == 0:
            act = _even_layer(act, layer, idx, p, batch_major_in=(layer == 0))
        else:
            act = _odd_layer(act, layer, idx, p, final_norm=(layer == depth - 1),
                             out_dtype=x.dtype)
    return act
```

```python
import functools
import math

import jax
import jax.numpy as jnp
from jax import lax
from jax.experimental import pallas as pl
from jax.experimental.pallas import tpu as pltpu

F32 = jnp.float32
BF16 = jnp.bfloat16

RMS_EPS = 1e-6
S5_GROUP = 16
S5_STATE = 64
POOL_WINDOWS = (2, 4, 8, 16)
CONV_WIDTH = 3

SUBLANES = 8
S5_CHUNK = 128
GROUPS_PER_CHUNK = S5_CHUNK // S5_GROUP
STATES_PER_CHUNK = GROUPS_PER_CHUNK * S5_STATE
PAIR = 2

EVEN_TILE_STEPS = 64
ODD_TILE_STEPS = 64
CONV_CHUNK = 512
VMEM_LIMIT_BYTES = 56 * 1024 * 1024


def _dot(a, b):
    return jnp.dot(a, b, preferred_element_type=F32)


def _rmsnorm(x, g):
    ms = jnp.mean(x * x, axis=-1, keepdims=True)
    return x * lax.rsqrt(ms + RMS_EPS) * g


def _sigmoid(x):
    return 1.0 / (1.0 + jnp.exp(-x))


def _silu(x):
    return x * _sigmoid(x)


def _gelu_tanh(x):
    c = math.sqrt(2.0 / math.pi)
    return 0.5 * x * (1.0 + jnp.tanh(c * (x + 0.044715 * (x * x * x))))


def _load_rows(x_ref):
    if len(x_ref.shape) == 2:
        return x_ref[...]
    b, steps, d = x_ref.shape
    return jnp.swapaxes(x_ref[...], 0, 1).reshape(steps * b, d)


def _store_rows(o_ref, rows):
    if len(o_ref.shape) == 2:
        o_ref[...] = rows.astype(o_ref.dtype)
    else:
        b, steps, d = o_ref.shape
        o_ref[...] = jnp.swapaxes(rows.reshape(steps, b, d), 0, 1).astype(o_ref.dtype)


def _row_spec(m, d, batch_major):
    if batch_major:
        return pl.BlockSpec((SUBLANES, m // SUBLANES, d), lambda i: (0, i, 0))
    return pl.BlockSpec((m, d), lambda i: (i, 0))


def _layer_spec(stacked, layer):
    nd = stacked.ndim - 1
    return pl.BlockSpec((None,) + stacked.shape[1:], lambda i: (layer,) + (0,) * nd,
                        pipeline_mode=pl.Buffered(1))


def _cmul(ar, ai, br, bi):
    return ar * br - ai * bi, ar * bi + ai * br


def _dot_nt_exact(a, b):
    return lax.dot_general(a, b, (((1,), (1,)), ((), ())),
                           precision=lax.Precision.HIGHEST, preferred_element_type=F32)


def _s5_prep_kernel(are_ref, aim_ref, logdt_ref, bre_ref, bim_ref, cre_ref, cim_ref,
                    lre_ref, lim_ref, bmat_ref, cmat_ref, kmat_ref):
    nc, rows, lanes = are_ref.shape
    sp = STATES_PER_CHUNK
    reps = sp // lanes
    lam_re = are_ref[...]
    lam_im = aim_ref[...]
    dt = jnp.exp(logdt_ref[...])
    mag = jnp.exp(lam_re * dt)
    ang = lam_im * dt
    lb_re = mag * jnp.cos(ang)
    lb_im = mag * jnp.sin(ang)
    den = lam_re * lam_re + lam_im * lam_im
    f_re = ((lb_re - 1.0) * lam_re + lb_im * lam_im) / den
    f_im = (lb_im * lam_re - (lb_re - 1.0) * lam_im) / den
    bb_re, bb_im = _cmul(f_re, f_im, bre_ref[...], bim_ref[...])
    lb2_re, lb2_im = _cmul(lb_re, lb_im, lb_re, lb_im)
    lbb_re, lbb_im = _cmul(lb_re, lb_im, bb_re, bb_im)
    c_re = cre_ref[...]
    c_im = cim_ref[...]
    cl_re, cl_im = _cmul(c_re, c_im, lb_re, lb_im)
    cl2_re, cl2_im = _cmul(c_re, c_im, lb2_re, lb2_im)

    row_g = lax.broadcasted_iota(jnp.int32, (rows, sp), 0) // S5_GROUP
    col_g = lax.broadcasted_iota(jnp.int32, (rows, sp), 1) // S5_STATE
    same_group = row_g == col_g
    first_row = lax.broadcasted_iota(jnp.int32, (rows, sp), 0) % S5_GROUP == 0
    krow_g = lax.broadcasted_iota(jnp.int32, (rows, rows), 0) // S5_GROUP
    kcol_g = lax.broadcasted_iota(jnp.int32, (rows, rows), 1) // S5_GROUP
    k_same_group = krow_g == kcol_g

    def tiled(v):
        return jnp.concatenate([v] * reps, axis=1)

    def block_diag(v):
        return jnp.where(same_group, tiled(v), 0.0)

    def group_row(v):
        r = jnp.sum(jnp.where(same_group & first_row, tiled(v), 0.0), axis=0, keepdims=True)
        return jnp.broadcast_to(r, (SUBLANES, sp))

    def direct(a_re, a_im, b_re, b_im):
        k = _dot_nt_exact(a_re, b_re) - _dot_nt_exact(a_im, b_im)
        return jnp.where(k_same_group, k * (1.0 / (lanes // S5_STATE)), 0.0)

    c = S5_CHUNK
    for j in range(nc):
        lre_ref[j] = group_row(lb2_re[j])
        lim_ref[j] = group_row(lb2_im[j])
        bmat_ref[j, 0:c, 0:sp] = block_diag(lbb_re[j]).astype(BF16)
        bmat_ref[j, 0:c, sp:2 * sp] = block_diag(lbb_im[j]).astype(BF16)
        bmat_ref[j, c:2 * c, 0:sp] = block_diag(bb_re[j]).astype(BF16)
        bmat_ref[j, c:2 * c, sp:2 * sp] = block_diag(bb_im[j]).astype(BF16)
        cmat_ref[j, 0:sp, 0:c] = block_diag(cl_re[j]).T.astype(BF16)
        cmat_ref[j, 0:sp, c:2 * c] = block_diag(cl2_re[j]).T.astype(BF16)
        cmat_ref[j, sp:2 * sp, 0:c] = block_diag(-cl_im[j]).T.astype(BF16)
        cmat_ref[j, sp:2 * sp, c:2 * c] = block_diag(-cl2_im[j]).T.astype(BF16)
        k0 = direct(bb_re[j], bb_im[j], c_re[j], c_im[j]).astype(BF16)
        k1 = direct(bb_re[j], bb_im[j], cl_re[j], cl_im[j]).astype(BF16)
        kmat_ref[j, 0:c, 0:c] = k0
        kmat_ref[j, 0:c, c:2 * c] = k1
        kmat_ref[j, c:2 * c, 0:c] = jnp.zeros((c, c), BF16)
        kmat_ref[j, c:2 * c, c:2 * c] = k0


def _s5_prep(a_re, a_im, log_dt, b_re, b_im, c_re, c_im):
    n, g, p = a_re.shape
    h = b_re.shape[-1]
    nc = g // GROUPS_PER_CHUNK
    sp = STATES_PER_CHUNK
    lanes = 2 * p
    assert lanes == 128 and sp % lanes == 0

    def chunked(v):
        v = jnp.broadcast_to(v, (n, g, h, p))
        return jnp.concatenate([v, v], axis=-1).reshape(n, nc, S5_CHUNK, lanes)

    args = (chunked(a_re[:, :, None, :]), chunked(a_im[:, :, None, :]),
            chunked(log_dt[:, :, None, None]),
            chunked(jnp.swapaxes(b_re, 2, 3)), chunked(jnp.swapaxes(b_im, 2, 3)),
            chunked(c_re), chunked(c_im))
    in_spec = pl.BlockSpec((None, nc, S5_CHUNK, lanes), lambda i: (i, 0, 0, 0))

    def out(shape, dtype):
        return (pl.BlockSpec((None,) + shape, lambda i: (i,) + (0,) * len(shape)),
                jax.ShapeDtypeStruct((n,) + shape, dtype))

    outs = [out((nc, SUBLANES, sp), F32), out((nc, SUBLANES, sp), F32),
            out((nc, 2 * S5_CHUNK, 2 * sp), BF16), out((nc, 2 * sp, 2 * S5_CHUNK), BF16),
            out((nc, 2 * S5_CHUNK, 2 * S5_CHUNK), BF16)]
    return pl.pallas_call(
        _s5_prep_kernel,
        grid=(n,),
        in_specs=[in_spec] * len(args),
        out_specs=[o[0] for o in outs],
        out_shape=[o[1] for o in outs],
        name="s5_prep",
    )(*args)


def _even_kernel(x_ref, g_ref, win_ref, bmat_ref, lre_ref, lim_ref, cmat_ref, kmat_ref,
                 d_ref, wglu_ref, bglu_ref, wpool_ref, pscale_ref, wout_ref,
                 o_ref,
                 st_re, st_im, ua_scr, l_scr, e_scr, ya_scr, ub_buf):
    i = pl.program_id(0)
    pairs, prow, d = ya_scr.shape
    m = pairs * prow
    mp = pairs * SUBLANES
    steps = m // SUBLANES
    sp = STATES_PER_CHUNK
    n_chunks = d // S5_CHUNK
    hist = ub_buf.shape[0] - m

    @pl.when(i == 0)
    def _():
        st_re[...] = jnp.zeros_like(st_re)
        st_im[...] = jnp.zeros_like(st_im)
        ub_buf[0:hist, :] = jnp.zeros((hist, d), F32)

    x = _load_rows(x_ref)
    h = _rmsnorm(x, g_ref[...]).astype(BF16)

    u_a = _dot(h, win_ref[:, 0:d])
    ua_scr[...] = u_a.reshape(pairs, prow, d)
    for j in range(n_chunks):
        slot = j % 2
        c0, c1 = j * S5_CHUNK, (j + 1) * S5_CHUNK
        u_first = ua_scr[:, 0:SUBLANES, c0:c1].reshape(mp, S5_CHUNK)
        u_second = ua_scr[:, SUBLANES:prow, c0:c1].reshape(mp, S5_CHUNK)
        u_pair = jnp.concatenate([u_first, u_second], axis=1).astype(BF16)
        l_scr[slot] = _dot(u_pair, bmat_ref[j])
        l_re = lre_ref[j]
        l_im = lim_ref[j]
        s_re = st_re[j]
        s_im = st_im[j]
        for c in range(pairs):
            r0, r1 = c * SUBLANES, (c + 1) * SUBLANES
            e_scr[slot, r0:r1, 0:sp] = s_re
            e_scr[slot, r0:r1, sp:2 * sp] = s_im
            n_re = l_re * s_re - l_im * s_im + l_scr[slot, r0:r1, 0:sp]
            n_im = l_re * s_im + l_im * s_re + l_scr[slot, r0:r1, sp:2 * sp]
            s_re, s_im = n_re, n_im
        st_re[j] = s_re
        st_im[j] = s_im
        y_pair = _dot(e_scr[slot].astype(BF16), cmat_ref[j]) + _dot(u_pair, kmat_ref[j])
        ya_scr[:, 0:SUBLANES, c0:c1] = y_pair[:, 0:S5_CHUNK].reshape(pairs, SUBLANES, S5_CHUNK)
        ya_scr[:, SUBLANES:prow, c0:c1] = y_pair[:, S5_CHUNK:2 * S5_CHUNK].reshape(
            pairs, SUBLANES, S5_CHUNK)

    y_a = _gelu_tanh(ya_scr[...].reshape(m, d) + d_ref[...] * u_a)
    z_glu = _dot(y_a.astype(BF16), wglu_ref[...]) + bglu_ref[...]
    g_a = _dot(h, win_ref[:, d:2 * d])
    y_a = z_glu[:, 0:d] * _sigmoid(z_glu[:, d:2 * d]) * _silu(g_a)
    acc = _dot(y_a.astype(BF16), wout_ref[0:d, :])

    u_b = _dot(h, win_ref[:, 2 * d:3 * d])
    ub_buf[hist:hist + m, :] = u_b
    pg = d // len(POOL_WINDOWS)
    t_idx = i * steps + (lax.broadcasted_iota(jnp.int32, (m, pg), 0) // SUBLANES)
    parts = []
    for gi, w in enumerate(POOL_WINDOWS):
        c0, c1 = gi * pg, (gi + 1) * pg
        cur = u_b[:, c0:c1]
        win_sum = cur
        for k in range(1, w):
            off = hist - k * SUBLANES
            win_sum = win_sum + ub_buf[off:off + m, c0:c1]
        count = jnp.minimum(t_idx + 1, w).astype(F32)
        diff = win_sum / count - cur
        parts.append(_dot(diff.astype(BF16), wpool_ref[gi]))
    ub_buf[0:hist, :] = ub_buf[m:m + hist, :]
    g_b = _dot(h, win_ref[:, 3 * d:4 * d])
    y_b = jnp.concatenate(parts, axis=1) * pscale_ref[...] * _silu(g_b)
    acc = acc + _dot(y_b.astype(BF16), wout_ref[d:2 * d, :])

    _store_rows(o_ref, x + acc)


def _even_layer(x_in, layer, idx, p, batch_major_in):
    d = x_in.shape[-1]
    rows = x_in.size // d
    assert EVEN_TILE_STEPS % PAIR == 0
    m = EVEN_TILE_STEPS * SUBLANES
    pairs = EVEN_TILE_STEPS // PAIR
    mp = pairs * SUBLANES
    hist = max(POOL_WINDOWS) * SUBLANES
    nc = d // S5_CHUNK
    sp = STATES_PER_CHUNK
    stacked = (p["norm_g"], p["ev_w_in"], p["b_mat"], p["lam_re"], p["lam_im"], p["c_mat"],
               p["k_mat"], p["s5_d"], p["s5_w_glu"], p["s5_b_glu"], p["pool_w"],
               p["pool_scale"], p["ev_w_out"])
    idxs = (layer,) + (idx,) * (len(stacked) - 1)
    in_specs = [_row_spec(m, d, batch_major_in)]
    in_specs += [_layer_spec(a, k) for a, k in zip(stacked, idxs)]
    return pl.pallas_call(
        _even_kernel,
        grid=(rows // m,),
        in_specs=in_specs,
        out_specs=_row_spec(m, d, False),
        out_shape=jax.ShapeDtypeStruct((rows, d), F32),
        scratch_shapes=[
            pltpu.VMEM((nc, SUBLANES, sp), F32),
            pltpu.VMEM((nc, SUBLANES, sp), F32),
            pltpu.VMEM((pairs, PAIR * SUBLANES, d), F32),
            pltpu.VMEM((2, mp, 2 * sp), F32),
            pltpu.VMEM((2, mp, 2 * sp), F32),
            pltpu.VMEM((pairs, PAIR * SUBLANES, d), F32),
            pltpu.VMEM((hist + m, d), F32),
        ],
        compiler_params=pltpu.CompilerParams(
            dimension_semantics=("arbitrary",),
            vmem_limit_bytes=VMEM_LIMIT_BYTES),
        name="even_layer",
    )(x_in, *stacked)


def _odd_kernel(x_ref, g_ref, win_ref, cw_ref, cb_ref, wout_ref, fg_ref, o_ref,
                v_buf, *, final_norm):
    i = pl.program_id(0)
    dc, d = wout_ref.shape
    n_chunks = dc // CONV_CHUNK
    hist = (CONV_WIDTH - 1) * SUBLANES
    m = v_buf.shape[1] - hist

    @pl.when(i == 0)
    def _():
        v_buf[:, 0:hist, :] = jnp.zeros((n_chunks, hist, CONV_CHUNK), F32)

    x = _load_rows(x_ref)
    h = _rmsnorm(x, g_ref[...]).astype(BF16)
    acc = jnp.zeros((m, d), F32)
    for c in range(n_chunks):
        c0, c1 = c * CONV_CHUNK, (c + 1) * CONV_CHUNK
        x_in = _dot(h, win_ref[:, c0:c1])
        c_gate = _dot(h, win_ref[:, 2 * dc + c0:2 * dc + c1])
        v = c_gate * x_in
        v_buf[c, hist:hist + m, :] = v
        conv = cb_ref[:, c0:c1] + cw_ref[CONV_WIDTH - 1:CONV_WIDTH, c0:c1] * v
        for k in range(1, CONV_WIDTH):
            off = hist - k * SUBLANES
            tap = CONV_WIDTH - 1 - k
            conv = conv + cw_ref[tap:tap + 1, c0:c1] * v_buf[c, off:off + m, :]
        v_buf[c, 0:hist, :] = v_buf[c, m:m + hist, :]
        b_gate = _dot(h, win_ref[:, dc + c0:dc + c1])
        gate = _dot(h, win_ref[:, 3 * dc + c0:3 * dc + c1])
        y = b_gate * conv * _silu(gate)
        acc = acc + _dot(y.astype(BF16), wout_ref[c0:c1, :])
    out = x + acc
    if final_norm:
        out = _rmsnorm(out, fg_ref[...])
    _store_rows(o_ref, out)


def _odd_layer(xt, layer, idx, p, final_norm, out_dtype):
    rows, d = xt.shape
    dc = p["sc_w_out"].shape[1]
    m = ODD_TILE_STEPS * SUBLANES
    hist = (CONV_WIDTH - 1) * SUBLANES
    stacked = (p["norm_g"], p["sc_w_in"], p["sc_conv_w"], p["sc_conv_b"], p["sc_w_out"],
               p["final_g"])
    idxs = (layer, idx, idx, idx, idx, 0)
    in_specs = [_row_spec(m, d, False)]
    in_specs += [_layer_spec(a, k) for a, k in zip(stacked, idxs)]
    if final_norm:
        out_shape = jax.ShapeDtypeStruct((SUBLANES, rows // SUBLANES, d), out_dtype)
    else:
        out_shape = jax.ShapeDtypeStruct((rows, d), F32)
    return pl.pallas_call(
        functools.partial(_odd_kernel, final_norm=final_norm),
        grid=(rows // m,),
        in_specs=in_specs,
        out_specs=_row_spec(m, d, final_norm),
        out_shape=out_shape,
        scratch_shapes=[
            pltpu.VMEM((dc // CONV_CHUNK, hist + m, CONV_CHUNK), F32),
        ],
        compiler_params=pltpu.CompilerParams(
            dimension_semantics=("arbitrary",),
            vmem_limit_bytes=VMEM_LIMIT_BYTES),
        name="odd_layer_final" if final_norm else "odd_layer",
    )(xt, *stacked)


def kernel(x, norm_g, final_g, ev_w_in, ev_w_out, s5_a_re, s5_a_im, s5_log_dt, s5_b_re, s5_b_im,
           s5_c_re, s5_c_im, s5_d, s5_w_glu, s5_b_glu, pool_w, pool_scale, sc_w_in, sc_conv_w,
           sc_conv_b, sc_w_out):
    bsz, seq, d = x.shape
    assert bsz == SUBLANES, "time-major tiling assumes one timestep per 8-row tile"
    depth = norm_g.shape[0]
    assert depth % 2 == 0, "the final norm is fused into the last (odd) layer"

    lam_re, lam_im, b_mat, c_mat, k_mat = _s5_prep(
        s5_a_re, s5_a_im, s5_log_dt, s5_b_re, s5_b_im, s5_c_re, s5_c_im)
    p = {
        "norm_g": norm_g[:, None, :], "final_g": final_g[None, None, :],
        "ev_w_in": ev_w_in.astype(BF16), "ev_w_out": ev_w_out.astype(BF16),
        "lam_re": lam_re, "lam_im": lam_im, "b_mat": b_mat, "c_mat": c_mat, "k_mat": k_mat,
        "s5_d": s5_d[:, None, :], "s5_w_glu": s5_w_glu.astype(BF16),
        "s5_b_glu": s5_b_glu[:, None, :], "pool_w": pool_w.astype(BF16),
        "pool_scale": pool_scale[:, None, :],
        "sc_w_in": sc_w_in.astype(BF16), "sc_conv_w": sc_conv_w,
        "sc_conv_b": sc_conv_b[:, None, :], "sc_w_out": sc_w_out.astype(BF16),
    }

    act = x.astype(F32)
    for layer in range(depth):
        idx = layer // 2
        if layer % 2 == 0:
            act = _even_layer(act, layer, idx, p, batch_major_in=(layer == 0))
        else:
            act = _odd_layer(act, layer, idx, p, final_norm=(layer == depth - 1),
                             out_dtype=x.dtype)
    return act
```

```python
import functools
import math

import jax
import jax.numpy as jnp
from jax import lax
from jax.experimental import pallas as pl
from jax.experimental.pallas import tpu as pltpu

F32 = jnp.float32
BF16 = jnp.bfloat16

RMS_EPS = 1e-6
S5_GROUP = 16
S5_STATE = 64
POOL_WINDOWS = (2, 4, 8, 16)
CONV_WIDTH = 3

SUBLANES = 8
S5_CHUNK = 128
GROUPS_PER_CHUNK = S5_CHUNK // S5_GROUP
STATES_PER_CHUNK = GROUPS_PER_CHUNK * S5_STATE
PAIR = 2

EVEN_TILE_STEPS = 64
ODD_TILE_STEPS = 64
CONV_CHUNK = 512
VMEM_LIMIT_BYTES = 56 * 1024 * 1024


def _dot(a, b):
    return jnp.dot(a, b, preferred_element_type=F32)


def _rmsnorm(x, g):
    ms = jnp.mean(x * x, axis=-1, keepdims=True)
    return x * lax.rsqrt(ms + RMS_EPS) * g


def _sigmoid(x):
    return 1.0 / (1.0 + jnp.exp(-x))


def _silu(x):
    return x * _sigmoid(x)


def _gelu_tanh(x):
    c = math.sqrt(2.0 / math.pi)
    return 0.5 * x * (1.0 + jnp.tanh(c * (x + 0.044715 * (x * x * x))))


def _load_rows(x_ref):
    if len(x_ref.shape) == 2:
        return x_ref[...]
    b, steps, d = x_ref.shape
    return jnp.swapaxes(x_ref[...], 0, 1).reshape(steps * b, d)


def _store_rows(o_ref, rows):
    if len(o_ref.shape) == 2:
        o_ref[...] = rows.astype(o_ref.dtype)
    else:
        b, steps, d = o_ref.shape
        o_ref[...] = jnp.swapaxes(rows.reshape(steps, b, d), 0, 1).astype(o_ref.dtype)


def _row_spec(m, d, batch_major):
    if batch_major:
        return pl.BlockSpec((SUBLANES, m // SUBLANES, d), lambda i: (0, i, 0))
    return pl.BlockSpec((m, d), lambda i: (i, 0))


def _resident_spec(param):
    arr, layer = param
    if layer is None:
        nd = arr.ndim
        return pl.BlockSpec(arr.shape, lambda i: (0,) * nd, pipeline_mode=pl.Buffered(1))
    nd = arr.ndim - 1
    return pl.BlockSpec((None,) + arr.shape[1:], lambda i: (layer,) + (0,) * nd,
                        pipeline_mode=pl.Buffered(1))


def _cast_specs(casts, n_steps):
    in_specs, out_specs, out_shapes = [], [], []
    for arr, layer in casts:
        r, c = arr.shape[1:]
        assert r % (n_steps * 16) == 0, "row band must hold whole bf16 tiles"
        band = r // n_steps
        in_specs.append(pl.BlockSpec((None, band, c), lambda i, layer=layer: (layer, i, 0)))
        out_specs.append(pl.BlockSpec((band, c), lambda i: (i, 0)))
        out_shapes.append(jax.ShapeDtypeStruct((r, c), BF16))
    return in_specs, out_specs, out_shapes


def _cast_blocks(src_refs, dst_refs):
    for src, dst in zip(src_refs, dst_refs):
        dst[...] = src[...].astype(dst.dtype)


def _cmul(ar, ai, br, bi):
    return ar * br - ai * bi, ar * bi + ai * br


def _dot_nt_exact(a, b):
    return lax.dot_general(a, b, (((1,), (1,)), ((), ())),
                           precision=lax.Precision.HIGHEST, preferred_element_type=F32)


def _s5_prep_kernel(are_ref, aim_ref, logdt_ref, bre_ref, bim_ref, cre_ref, cim_ref,
                    lre_ref, lim_ref, bmat_ref, cmat_ref, kmat_ref):
    nc, rows, lanes = are_ref.shape
    sp = STATES_PER_CHUNK
    reps = sp // lanes
    lam_re = are_ref[...]
    lam_im = aim_ref[...]
    dt = jnp.exp(logdt_ref[...])
    mag = jnp.exp(lam_re * dt)
    ang = lam_im * dt
    lb_re = mag * jnp.cos(ang)
    lb_im = mag * jnp.sin(ang)
    den = lam_re * lam_re + lam_im * lam_im
    f_re = ((lb_re - 1.0) * lam_re + lb_im * lam_im) / den
    f_im = (lb_im * lam_re - (lb_re - 1.0) * lam_im) / den
    bb_re, bb_im = _cmul(f_re, f_im, bre_ref[...], bim_ref[...])
    lb2_re, lb2_im = _cmul(lb_re, lb_im, lb_re, lb_im)
    lbb_re, lbb_im = _cmul(lb_re, lb_im, bb_re, bb_im)
    c_re = cre_ref[...]
    c_im = cim_ref[...]
    cl_re, cl_im = _cmul(c_re, c_im, lb_re, lb_im)
    cl2_re, cl2_im = _cmul(c_re, c_im, lb2_re, lb2_im)

    row_g = lax.broadcasted_iota(jnp.int32, (rows, sp), 0) // S5_GROUP
    col_g = lax.broadcasted_iota(jnp.int32, (rows, sp), 1) // S5_STATE
    same_group = row_g == col_g
    first_row = lax.broadcasted_iota(jnp.int32, (rows, sp), 0) % S5_GROUP == 0
    krow_g = lax.broadcasted_iota(jnp.int32, (rows, rows), 0) // S5_GROUP
    kcol_g = lax.broadcasted_iota(jnp.int32, (rows, rows), 1) // S5_GROUP
    k_same_group = krow_g == kcol_g

    def tiled(v):
        return jnp.concatenate([v] * reps, axis=1)

    def block_diag(v):
        return jnp.where(same_group, tiled(v), 0.0)

    def group_row(v):
        r = jnp.sum(jnp.where(same_group & first_row, tiled(v), 0.0), axis=0, keepdims=True)
        return jnp.broadcast_to(r, (SUBLANES, sp))

    def direct(a_re, a_im, b_re, b_im):
        k = _dot_nt_exact(a_re, b_re) - _dot_nt_exact(a_im, b_im)
        return jnp.where(k_same_group, k * (1.0 / (lanes // S5_STATE)), 0.0)

    c = S5_CHUNK
    for j in range(nc):
        lre_ref[j] = group_row(lb2_re[j])
        lim_ref[j] = group_row(lb2_im[j])
        bmat_ref[j, 0:c, 0:sp] = block_diag(lbb_re[j]).astype(BF16)
        bmat_ref[j, 0:c, sp:2 * sp] = block_diag(lbb_im[j]).astype(BF16)
        bmat_ref[j, c:2 * c, 0:sp] = block_diag(bb_re[j]).astype(BF16)
        bmat_ref[j, c:2 * c, sp:2 * sp] = block_diag(bb_im[j]).astype(BF16)
        cmat_ref[j, 0:sp, 0:c] = block_diag(cl_re[j]).T.astype(BF16)
        cmat_ref[j, 0:sp, c:2 * c] = block_diag(cl2_re[j]).T.astype(BF16)
        cmat_ref[j, sp:2 * sp, 0:c] = block_diag(-cl_im[j]).T.astype(BF16)
        cmat_ref[j, sp:2 * sp, c:2 * c] = block_diag(-cl2_im[j]).T.astype(BF16)
        k0 = direct(bb_re[j], bb_im[j], c_re[j], c_im[j]).astype(BF16)
        k1 = direct(bb_re[j], bb_im[j], cl_re[j], cl_im[j]).astype(BF16)
        kmat_ref[j, 0:c, 0:c] = k0
        kmat_ref[j, 0:c, c:2 * c] = k1
        kmat_ref[j, c:2 * c, 0:c] = jnp.zeros((c, c), BF16)
        kmat_ref[j, c:2 * c, c:2 * c] = k0


def _s5_prep(a_re, a_im, log_dt, b_re, b_im, c_re, c_im):
    n, g, p = a_re.shape
    h = b_re.shape[-1]
    nc = g // GROUPS_PER_CHUNK
    sp = STATES_PER_CHUNK
    lanes = 2 * p
    assert lanes == 128 and sp % lanes == 0

    def chunked(v):
        v = jnp.broadcast_to(v, (n, g, h, p))
        return jnp.concatenate([v, v], axis=-1).reshape(n, nc, S5_CHUNK, lanes)

    args = (chunked(a_re[:, :, None, :]), chunked(a_im[:, :, None, :]),
            chunked(log_dt[:, :, None, None]),
            chunked(jnp.swapaxes(b_re, 2, 3)), chunked(jnp.swapaxes(b_im, 2, 3)),
            chunked(c_re), chunked(c_im))
    in_spec = pl.BlockSpec((None, nc, S5_CHUNK, lanes), lambda i: (i, 0, 0, 0))

    def out(shape, dtype):
        return (pl.BlockSpec((None,) + shape, lambda i: (i,) + (0,) * len(shape)),
                jax.ShapeDtypeStruct((n,) + shape, dtype))

    outs = [out((nc, SUBLANES, sp), F32), out((nc, SUBLANES, sp), F32),
            out((nc, 2 * S5_CHUNK, 2 * sp), BF16), out((nc, 2 * sp, 2 * S5_CHUNK), BF16),
            out((nc, 2 * S5_CHUNK, 2 * S5_CHUNK), BF16)]
    return pl.pallas_call(
        _s5_prep_kernel,
        grid=(n,),
        in_specs=[in_spec] * len(args),
        out_specs=[o[0] for o in outs],
        out_shape=[o[1] for o in outs],
        name="s5_prep",
    )(*args)


EVEN_INPUTS = 14


def _even_kernel(*refs, n_cast):
    (x_ref, g_ref, win_ref, bmat_ref, lre_ref, lim_ref, cmat_ref, kmat_ref,
     d_ref, wglu_ref, bglu_ref, wpool_ref, pscale_ref, wout_ref) = refs[:EVEN_INPUTS]
    cast_src = refs[EVEN_INPUTS:EVEN_INPUTS + n_cast]
    o_ref = refs[EVEN_INPUTS + n_cast]
    cast_dst = refs[EVEN_INPUTS + n_cast + 1:EVEN_INPUTS + 2 * n_cast + 1]
    st_re, st_im, ua_scr, l_scr, e_scr, ya_scr, ub_buf = refs[EVEN_INPUTS + 2 * n_cast + 1:]
    _cast_blocks(cast_src, cast_dst)
    i = pl.program_id(0)
    pairs, prow, d = ya_scr.shape
    m = pairs * prow
    mp = pairs * SUBLANES
    steps = m // SUBLANES
    sp = STATES_PER_CHUNK
    n_chunks = d // S5_CHUNK
    hist = ub_buf.shape[0] - m

    @pl.when(i == 0)
    def _():
        st_re[...] = jnp.zeros_like(st_re)
        st_im[...] = jnp.zeros_like(st_im)
        ub_buf[0:hist, :] = jnp.zeros((hist, d), F32)

    x = _load_rows(x_ref)
    h = _rmsnorm(x, g_ref[...]).astype(BF16)

    u_a = _dot(h, win_ref[:, 0:d])
    ua_scr[...] = u_a.reshape(pairs, prow, d)
    for j in range(n_chunks):
        slot = j % 2
        c0, c1 = j * S5_CHUNK, (j + 1) * S5_CHUNK
        u_first = ua_scr[:, 0:SUBLANES, c0:c1].reshape(mp, S5_CHUNK)
        u_second = ua_scr[:, SUBLANES:prow, c0:c1].reshape(mp, S5_CHUNK)
        u_pair = jnp.concatenate([u_first, u_second], axis=1).astype(BF16)
        l_scr[slot] = _dot(u_pair, bmat_ref[j])
        l_re = lre_ref[j]
        l_im = lim_ref[j]
        s_re = st_re[j]
        s_im = st_im[j]
        for c in range(pairs):
            r0, r1 = c * SUBLANES, (c + 1) * SUBLANES
            e_scr[slot, r0:r1, 0:sp] = s_re
            e_scr[slot, r0:r1, sp:2 * sp] = s_im
            n_re = l_re * s_re - l_im * s_im + l_scr[slot, r0:r1, 0:sp]
            n_im = l_re * s_im + l_im * s_re + l_scr[slot, r0:r1, sp:2 * sp]
            s_re, s_im = n_re, n_im
        st_re[j] = s_re
        st_im[j] = s_im
        y_pair = _dot(e_scr[slot].astype(BF16), cmat_ref[j]) + _dot(u_pair, kmat_ref[j])
        ya_scr[:, 0:SUBLANES, c0:c1] = y_pair[:, 0:S5_CHUNK].reshape(pairs, SUBLANES, S5_CHUNK)
        ya_scr[:, SUBLANES:prow, c0:c1] = y_pair[:, S5_CHUNK:2 * S5_CHUNK].reshape(
            pairs, SUBLANES, S5_CHUNK)

    y_a = _gelu_tanh(ya_scr[...].reshape(m, d) + d_ref[...] * u_a)
    z_glu = _dot(y_a.astype(BF16), wglu_ref[...]) + bglu_ref[...]
    g_a = _dot(h, win_ref[:, d:2 * d])
    y_a = z_glu[:, 0:d] * _sigmoid(z_glu[:, d:2 * d]) * _silu(g_a)
    acc = _dot(y_a.astype(BF16), wout_ref[0:d, :])

    u_b = _dot(h, win_ref[:, 2 * d:3 * d])
    ub_buf[hist:hist + m, :] = u_b
    pg = d // len(POOL_WINDOWS)
    t_idx = i * steps + (lax.broadcasted_iota(jnp.int32, (m, pg), 0) // SUBLANES)
    parts = []
    for gi, w in enumerate(POOL_WINDOWS):
        c0, c1 = gi * pg, (gi + 1) * pg
        cur = u_b[:, c0:c1]
        win_sum = cur
        for k in range(1, w):
            off = hist - k * SUBLANES
            win_sum = win_sum + ub_buf[off:off + m, c0:c1]
        count = jnp.minimum(t_idx + 1, w).astype(F32)
        diff = win_sum / count - cur
        parts.append(_dot(diff.astype(BF16), wpool_ref[gi]))
    ub_buf[0:hist, :] = ub_buf[m:m + hist, :]
    g_b = _dot(h, win_ref[:, 3 * d:4 * d])
    y_b = jnp.concatenate(parts, axis=1) * pscale_ref[...] * _silu(g_b)
    acc = acc + _dot(y_b.astype(BF16), wout_ref[d:2 * d, :])

    _store_rows(o_ref, x + acc)


def _even_layer(x_in, params, casts, batch_major_in):
    d = x_in.shape[-1]
    rows = x_in.size // d
    assert EVEN_TILE_STEPS % PAIR == 0
    m = EVEN_TILE_STEPS * SUBLANES
    pairs = EVEN_TILE_STEPS // PAIR
    mp = pairs * SUBLANES
    hist = max(POOL_WINDOWS) * SUBLANES
    nc = d // S5_CHUNK
    sp = STATES_PER_CHUNK
    n_steps = rows // m
    assert len(params) == EVEN_INPUTS - 1
    cast_in, cast_out, cast_shapes = _cast_specs(casts, n_steps)
    in_specs = [_row_spec(m, d, batch_major_in)] + [_resident_spec(q) for q in params] + cast_in
    return pl.pallas_call(
        functools.partial(_even_kernel, n_cast=len(casts)),
        grid=(n_steps,),
        in_specs=in_specs,
        out_specs=[_row_spec(m, d, False)] + cast_out,
        out_shape=[jax.ShapeDtypeStruct((rows, d), F32)] + cast_shapes,
        scratch_shapes=[
            pltpu.VMEM((nc, SUBLANES, sp), F32),
            pltpu.VMEM((nc, SUBLANES, sp), F32),
            pltpu.VMEM((pairs, PAIR * SUBLANES, d), F32),
            pltpu.VMEM((2, mp, 2 * sp), F32),
            pltpu.VMEM((2, mp, 2 * sp), F32),
            pltpu.VMEM((pairs, PAIR * SUBLANES, d), F32),
            pltpu.VMEM((hist + m, d), F32),
        ],
        compiler_params=pltpu.CompilerParams(
            dimension_semantics=("arbitrary",),
            vmem_limit_bytes=VMEM_LIMIT_BYTES),
        name="even_layer",
    )(x_in, *[q[0] for q in params], *[q[0] for q in casts])


ODD_INPUTS = 7


def _odd_kernel(*refs, n_cast, final_norm):
    x_ref, g_ref, win_ref, cw_ref, cb_ref, wout_ref, fg_ref = refs[:ODD_INPUTS]
    cast_src = refs[ODD_INPUTS:ODD_INPUTS + n_cast]
    o_ref = refs[ODD_INPUTS + n_cast]
    cast_dst = refs[ODD_INPUTS + n_cast + 1:ODD_INPUTS + 2 * n_cast + 1]
    (v_buf,) = refs[ODD_INPUTS + 2 * n_cast + 1:]
    _cast_blocks(cast_src, cast_dst)
    i = pl.program_id(0)
    dc, d = wout_ref.shape
    n_chunks = dc // CONV_CHUNK
    hist = (CONV_WIDTH - 1) * SUBLANES
    m = v_buf.shape[1] - hist

    @pl.when(i == 0)
    def _():
        v_buf[:, 0:hist, :] = jnp.zeros((n_chunks, hist, CONV_CHUNK), F32)

    x = _load_rows(x_ref)
    h = _rmsnorm(x, g_ref[...]).astype(BF16)
    acc = jnp.zeros((m, d), F32)
    for c in range(n_chunks):
        c0, c1 = c * CONV_CHUNK, (c + 1) * CONV_CHUNK
        x_in = _dot(h, win_ref[:, c0:c1])
        c_gate = _dot(h, win_ref[:, 2 * dc + c0:2 * dc + c1])
        v = c_gate * x_in
        v_buf[c, hist:hist + m, :] = v
        conv = cb_ref[:, c0:c1] + cw_ref[CONV_WIDTH - 1:CONV_WIDTH, c0:c1] * v
        for k in range(1, CONV_WIDTH):
            off = hist - k * SUBLANES
            tap = CONV_WIDTH - 1 - k
            conv = conv + cw_ref[tap:tap + 1, c0:c1] * v_buf[c, off:off + m, :]
        v_buf[c, 0:hist, :] = v_buf[c, m:m + hist, :]
        b_gate = _dot(h, win_ref[:, dc + c0:dc + c1])
        gate = _dot(h, win_ref[:, 3 * dc + c0:3 * dc + c1])
        y = b_gate * conv * _silu(gate)
        acc = acc + _dot(y.astype(BF16), wout_ref[c0:c1, :])
    out = x + acc
    if final_norm:
        out = _rmsnorm(out, fg_ref[...])
    _store_rows(o_ref, out)


def _odd_layer(xt, params, casts, dc, final_norm, out_dtype):
    rows, d = xt.shape
    m = ODD_TILE_STEPS * SUBLANES
    hist = (CONV_WIDTH - 1) * SUBLANES
    n_steps = rows // m
    assert len(params) == ODD_INPUTS - 1
    cast_in, cast_out, cast_shapes = _cast_specs(casts, n_steps)
    in_specs = [_row_spec(m, d, False)] + [_resident_spec(q) for q in params] + cast_in
    if final_norm:
        out_shape = jax.ShapeDtypeStruct((SUBLANES, rows // SUBLANES, d), out_dtype)
    else:
        out_shape = jax.ShapeDtypeStruct((rows, d), F32)
    return pl.pallas_call(
        functools.partial(_odd_kernel, n_cast=len(casts), final_norm=final_norm),
        grid=(n_steps,),
        in_specs=in_specs,
        out_specs=[_row_spec(m, d, final_norm)] + cast_out,
        out_shape=[out_shape] + cast_shapes,
        scratch_shapes=[
            pltpu.VMEM((dc // CONV_CHUNK, hist + m, CONV_CHUNK), F32),
        ],
        compiler_params=pltpu.CompilerParams(
            dimension_semantics=("arbitrary",),
            vmem_limit_bytes=VMEM_LIMIT_BYTES),
        name="odd_layer_final" if final_norm else "odd_layer",
    )(xt, *[q[0] for q in params], *[q[0] for q in casts])


def kernel(x, norm_g, final_g, ev_w_in, ev_w_out, s5_a_re, s5_a_im, s5_log_dt, s5_b_re, s5_b_im,
           s5_c_re, s5_c_im, s5_d, s5_w_glu, s5_b_glu, pool_w, pool_scale, sc_w_in, sc_conv_w,
           sc_conv_b, sc_w_out):
    bsz, seq, d = x.shape
    assert bsz == SUBLANES, "time-major tiling assumes one timestep per 8-row tile"
    depth = norm_g.shape[0]
    assert depth % 2 == 0, "the final norm is fused into the last (odd) layer"

    lam_re, lam_im, b_mat, c_mat, k_mat = _s5_prep(
        s5_a_re, s5_a_im, s5_log_dt, s5_b_re, s5_b_im, s5_c_re, s5_c_im)
    norm_g3 = norm_g[:, None, :]
    final_g3 = final_g[None, None, :]
    pool_shape = pool_w.shape[1:]
    mats_f32 = {
        "ev_w_in": ev_w_in, "s5_w_glu": s5_w_glu, "ev_w_out": ev_w_out,
        "pool_w": pool_w.reshape(pool_w.shape[0], -1, pool_w.shape[-1]),
        "sc_w_in": sc_w_in, "sc_w_out": sc_w_out,
    }
    even_mats = ("ev_w_in", "s5_w_glu", "pool_w", "ev_w_out")
    odd_mats = ("sc_w_in", "sc_w_out")
    mats = {k: (mats_f32[k][0].astype(BF16), None) for k in even_mats}

    act = x.astype(F32)
    for layer in range(depth):
        idx = layer // 2
        nxt = layer + 1
        nxt_names = () if nxt == depth else (even_mats if nxt % 2 == 0 else odd_mats)
        casts = [(mats_f32[k], nxt // 2) for k in nxt_names]
        if layer % 2 == 0:
            w_pool = (mats["pool_w"][0].reshape(pool_shape), None)
            params = [(norm_g3, layer), mats["ev_w_in"], (b_mat, idx), (lam_re, idx),
                      (lam_im, idx), (c_mat, idx), (k_mat, idx), (s5_d[:, None, :], idx),
                      mats["s5_w_glu"], (s5_b_glu[:, None, :], idx), w_pool,
                      (pool_scale[:, None, :], idx), mats["ev_w_out"]]
            act, *cast_out = _even_layer(act, params, casts, batch_major_in=(layer == 0))
        else:
            params = [(norm_g3, layer), mats["sc_w_in"], (sc_conv_w, idx),
                      (sc_conv_b[:, None, :], idx), mats["sc_w_out"], (final_g3, 0)]
            act, *cast_out = _odd_layer(act, params, casts, dc=sc_w_out.shape[1],
                                        final_norm=(nxt == depth), out_dtype=x.dtype)
        mats = {k: (w, None) for k, w in zip(nxt_names, cast_out)}
    return act
```

```python
import functools
import math

import jax
import jax.numpy as jnp
from jax import lax
from jax.experimental import pallas as pl
from jax.experimental.pallas import tpu as pltpu

F32 = jnp.float32
BF16 = jnp.bfloat16

RMS_EPS = 1e-6
S5_GROUP = 16
S5_STATE = 64
POOL_WINDOWS = (2, 4, 8, 16)
CONV_WIDTH = 3

SUBLANES = 8
S5_CHUNK = 128
GROUPS_PER_CHUNK = S5_CHUNK // S5_GROUP
STATES_PER_CHUNK = GROUPS_PER_CHUNK * S5_STATE
PAIR = 2

EVEN_TILE_STEPS = 64
EVEN_SUB_TILES = 1
ODD_TILE_STEPS = 64
ODD_SUB_TILES = 2
CONV_CHUNK = 1024
VMEM_LIMIT_BYTES = 56 * 1024 * 1024


def _dot(a, b):
    return jnp.dot(a, b, preferred_element_type=F32)


def _rmsnorm(x, g):
    ms = jnp.mean(x * x, axis=-1, keepdims=True)
    return x * lax.rsqrt(ms + RMS_EPS) * g


def _sigmoid(x):
    return 1.0 / (1.0 + jnp.exp(-x))


def _silu(x):
    return x * _sigmoid(x)


def _gelu_tanh(x):
    c = math.sqrt(2.0 / math.pi)
    return 0.5 * x * (1.0 + jnp.tanh(c * (x + 0.044715 * (x * x * x))))


def _load_rows(x_ref, sub, m):
    if len(x_ref.shape) == 2:
        return x_ref[sub * m:(sub + 1) * m, :]
    b, _, d = x_ref.shape
    steps = m // b
    return jnp.swapaxes(x_ref[:, sub * steps:(sub + 1) * steps, :], 0, 1).reshape(m, d)


def _store_rows(o_ref, rows, sub):
    m = rows.shape[0]
    if len(o_ref.shape) == 2:
        o_ref[sub * m:(sub + 1) * m, :] = rows.astype(o_ref.dtype)
    else:
        b, _, d = o_ref.shape
        steps = m // b
        o_ref[:, sub * steps:(sub + 1) * steps, :] = jnp.swapaxes(
            rows.reshape(steps, b, d), 0, 1).astype(o_ref.dtype)


def _row_spec(m, d, batch_major):
    if batch_major:
        return pl.BlockSpec((SUBLANES, m // SUBLANES, d), lambda i: (0, i, 0))
    return pl.BlockSpec((m, d), lambda i: (i, 0))


def _resident_spec(param):
    arr, layer = param
    if layer is None:
        nd = arr.ndim
        return pl.BlockSpec(arr.shape, lambda i: (0,) * nd, pipeline_mode=pl.Buffered(1))
    nd = arr.ndim - 1
    return pl.BlockSpec((None,) + arr.shape[1:], lambda i: (layer,) + (0,) * nd,
                        pipeline_mode=pl.Buffered(1))


def _cast_specs(casts, n_steps):
    in_specs, out_specs, out_shapes = [], [], []
    for arr, layer in casts:
        r, c = arr.shape[1:]
        assert r % (n_steps * 16) == 0, "row band must hold whole bf16 tiles"
        band = r // n_steps
        in_specs.append(pl.BlockSpec((None, band, c), lambda i, layer=layer: (layer, i, 0)))
        out_specs.append(pl.BlockSpec((band, c), lambda i: (i, 0)))
        out_shapes.append(jax.ShapeDtypeStruct((r, c), BF16))
    return in_specs, out_specs, out_shapes


def _cast_blocks(src_refs, dst_refs):
    for src, dst in zip(src_refs, dst_refs):
        dst[...] = src[...].astype(dst.dtype)


def _cmul(ar, ai, br, bi):
    return ar * br - ai * bi, ar * bi + ai * br


def _dot_nt_exact(a, b):
    return lax.dot_general(a, b, (((1,), (1,)), ((), ())),
                           precision=lax.Precision.HIGHEST, preferred_element_type=F32)


def _s5_prep_kernel(are_ref, aim_ref, logdt_ref, bre_ref, bim_ref, cre_ref, cim_ref,
                    lre_ref, lim_ref, bmat_ref, cmat_ref, kmat_ref):
    nc, rows, lanes = are_ref.shape
    sp = STATES_PER_CHUNK
    reps = sp // lanes
    lam_re = are_ref[...]
    lam_im = aim_ref[...]
    dt = jnp.exp(logdt_ref[...])
    mag = jnp.exp(lam_re * dt)
    ang = lam_im * dt
    lb_re = mag * jnp.cos(ang)
    lb_im = mag * jnp.sin(ang)
    den = lam_re * lam_re + lam_im * lam_im
    f_re = ((lb_re - 1.0) * lam_re + lb_im * lam_im) / den
    f_im = (lb_im * lam_re - (lb_re - 1.0) * lam_im) / den
    bb_re, bb_im = _cmul(f_re, f_im, bre_ref[...], bim_ref[...])
    lb2_re, lb2_im = _cmul(lb_re, lb_im, lb_re, lb_im)
    lbb_re, lbb_im = _cmul(lb_re, lb_im, bb_re, bb_im)
    c_re = cre_ref[...]
    c_im = cim_ref[...]
    cl_re, cl_im = _cmul(c_re, c_im, lb_re, lb_im)
    cl2_re, cl2_im = _cmul(c_re, c_im, lb2_re, lb2_im)

    row_g = lax.broadcasted_iota(jnp.int32, (rows, sp), 0) // S5_GROUP
    col_g = lax.broadcasted_iota(jnp.int32, (rows, sp), 1) // S5_STATE
    same_group = row_g == col_g
    first_row = lax.broadcasted_iota(jnp.int32, (rows, sp), 0) % S5_GROUP == 0
    krow_g = lax.broadcasted_iota(jnp.int32, (rows, rows), 0) // S5_GROUP
    kcol_g = lax.broadcasted_iota(jnp.int32, (rows, rows), 1) // S5_GROUP
    k_same_group = krow_g == kcol_g

    def tiled(v):
        return jnp.concatenate([v] * reps, axis=1)

    def block_diag(v):
        return jnp.where(same_group, tiled(v), 0.0)

    def group_row(v):
        r = jnp.sum(jnp.where(same_group & first_row, tiled(v), 0.0), axis=0, keepdims=True)
        return jnp.broadcast_to(r, (SUBLANES, sp))

    def direct(a_re, a_im, b_re, b_im):
        k = _dot_nt_exact(a_re, b_re) - _dot_nt_exact(a_im, b_im)
        return jnp.where(k_same_group, k * (1.0 / (lanes // S5_STATE)), 0.0)

    c = S5_CHUNK
    for j in range(nc):
        lre_ref[j] = group_row(lb2_re[j])
        lim_ref[j] = group_row(lb2_im[j])
        bmat_ref[j, 0:c, 0:sp] = block_diag(lbb_re[j]).astype(BF16)
        bmat_ref[j, 0:c, sp:2 * sp] = block_diag(lbb_im[j]).astype(BF16)
        bmat_ref[j, c:2 * c, 0:sp] = block_diag(bb_re[j]).astype(BF16)
        bmat_ref[j, c:2 * c, sp:2 * sp] = block_diag(bb_im[j]).astype(BF16)
        cmat_ref[j, 0:sp, 0:c] = block_diag(cl_re[j]).T.astype(BF16)
        cmat_ref[j, 0:sp, c:2 * c] = block_diag(cl2_re[j]).T.astype(BF16)
        cmat_ref[j, sp:2 * sp, 0:c] = block_diag(-cl_im[j]).T.astype(BF16)
        cmat_ref[j, sp:2 * sp, c:2 * c] = block_diag(-cl2_im[j]).T.astype(BF16)
        k0 = direct(bb_re[j], bb_im[j], c_re[j], c_im[j]).astype(BF16)
        k1 = direct(bb_re[j], bb_im[j], cl_re[j], cl_im[j]).astype(BF16)
        kmat_ref[j, 0:c, 0:c] = k0
        kmat_ref[j, 0:c, c:2 * c] = k1
        kmat_ref[j, c:2 * c, 0:c] = jnp.zeros((c, c), BF16)
        kmat_ref[j, c:2 * c, c:2 * c] = k0


def _s5_prep(a_re, a_im, log_dt, b_re, b_im, c_re, c_im):
    n, g, p = a_re.shape
    h = b_re.shape[-1]
    nc = g // GROUPS_PER_CHUNK
    sp = STATES_PER_CHUNK
    lanes = 2 * p
    assert lanes == 128 and sp % lanes == 0

    def chunked(v):
        v = jnp.broadcast_to(v, (n, g, h, p))
        return jnp.concatenate([v, v], axis=-1).reshape(n, nc, S5_CHUNK, lanes)

    args = (chunked(a_re[:, :, None, :]), chunked(a_im[:, :, None, :]),
            chunked(log_dt[:, :, None, None]),
            chunked(jnp.swapaxes(b_re, 2, 3)), chunked(jnp.swapaxes(b_im, 2, 3)),
            chunked(c_re), chunked(c_im))
    in_spec = pl.BlockSpec((None, nc, S5_CHUNK, lanes), lambda i: (i, 0, 0, 0))

    def out(shape, dtype):
        return (pl.BlockSpec((None,) + shape, lambda i: (i,) + (0,) * len(shape)),
                jax.ShapeDtypeStruct((n,) + shape, dtype))

    outs = [out((nc, SUBLANES, sp), F32), out((nc, SUBLANES, sp), F32),
            out((nc, 2 * S5_CHUNK, 2 * sp), BF16), out((nc, 2 * sp, 2 * S5_CHUNK), BF16),
            out((nc, 2 * S5_CHUNK, 2 * S5_CHUNK), BF16)]
    return pl.pallas_call(
        _s5_prep_kernel,
        grid=(n,),
        in_specs=[in_spec] * len(args),
        out_specs=[o[0] for o in outs],
        out_shape=[o[1] for o in outs],
        name="s5_prep",
    )(*args)


EVEN_INPUTS = 14


def _even_kernel(*refs, n_cast, sub_tiles):
    (x_ref, g_ref, win_ref, bmat_ref, lre_ref, lim_ref, cmat_ref, kmat_ref,
     d_ref, wglu_ref, bglu_ref, wpool_ref, pscale_ref, wout_ref) = refs[:EVEN_INPUTS]
    cast_src = refs[EVEN_INPUTS:EVEN_INPUTS + n_cast]
    o_ref = refs[EVEN_INPUTS + n_cast]
    cast_dst = refs[EVEN_INPUTS + n_cast + 1:EVEN_INPUTS + 2 * n_cast + 1]
    st_re, st_im, ua_scr, l_scr, e_scr, ya_scr, ub_buf = refs[EVEN_INPUTS + 2 * n_cast + 1:]
    _cast_blocks(cast_src, cast_dst)
    i = pl.program_id(0)
    pairs, prow, d = ya_scr.shape
    m = pairs * prow
    mp = pairs * SUBLANES
    steps = m // SUBLANES
    sp = STATES_PER_CHUNK
    n_chunks = d // S5_CHUNK
    hist = ub_buf.shape[0] - m

    @pl.when(i == 0)
    def _():
        st_re[...] = jnp.zeros_like(st_re)
        st_im[...] = jnp.zeros_like(st_im)
        ub_buf[0:hist, :] = jnp.zeros((hist, d), F32)

    for sub in range(sub_tiles):
        x = _load_rows(x_ref, sub, m)
        h = _rmsnorm(x, g_ref[...]).astype(BF16)

        u_a = _dot(h, win_ref[:, 0:d])
        ua_scr[...] = u_a.reshape(pairs, prow, d)
        for j in range(n_chunks):
            slot = j % 2
            c0, c1 = j * S5_CHUNK, (j + 1) * S5_CHUNK
            u_first = ua_scr[:, 0:SUBLANES, c0:c1].reshape(mp, S5_CHUNK)
            u_second = ua_scr[:, SUBLANES:prow, c0:c1].reshape(mp, S5_CHUNK)
            u_pair = jnp.concatenate([u_first, u_second], axis=1).astype(BF16)
            l_scr[slot] = _dot(u_pair, bmat_ref[j])
            l_re = lre_ref[j]
            l_im = lim_ref[j]
            s_re = st_re[j]
            s_im = st_im[j]
            for c in range(pairs):
                r0, r1 = c * SUBLANES, (c + 1) * SUBLANES
                e_scr[slot, r0:r1, 0:sp] = s_re
                e_scr[slot, r0:r1, sp:2 * sp] = s_im
                n_re = l_re * s_re - l_im * s_im + l_scr[slot, r0:r1, 0:sp]
                n_im = l_re * s_im + l_im * s_re + l_scr[slot, r0:r1, sp:2 * sp]
                s_re, s_im = n_re, n_im
            st_re[j] = s_re
            st_im[j] = s_im
            y_pair = _dot(e_scr[slot].astype(BF16), cmat_ref[j]) + _dot(u_pair, kmat_ref[j])
            ya_scr[:, 0:SUBLANES, c0:c1] = y_pair[:, 0:S5_CHUNK].reshape(pairs, SUBLANES, S5_CHUNK)
            ya_scr[:, SUBLANES:prow, c0:c1] = y_pair[:, S5_CHUNK:2 * S5_CHUNK].reshape(
                pairs, SUBLANES, S5_CHUNK)

        y_a = _gelu_tanh(ya_scr[...].reshape(m, d) + d_ref[...] * u_a)
        z_glu = _dot(y_a.astype(BF16), wglu_ref[...]) + bglu_ref[...]
        g_a = _dot(h, win_ref[:, d:2 * d])
        y_a = z_glu[:, 0:d] * _sigmoid(z_glu[:, d:2 * d]) * _silu(g_a)
        acc = _dot(y_a.astype(BF16), wout_ref[0:d, :])

        u_b = _dot(h, win_ref[:, 2 * d:3 * d])
        ub_buf[hist:hist + m, :] = u_b
        pg = d // len(POOL_WINDOWS)
        t_idx = (i * sub_tiles + sub) * steps + (lax.broadcasted_iota(jnp.int32, (m, pg), 0) // SUBLANES)
        parts = []
        for gi, w in enumerate(POOL_WINDOWS):
            c0, c1 = gi * pg, (gi + 1) * pg
            cur = u_b[:, c0:c1]
            win_sum = cur
            for k in range(1, w):
                off = hist - k * SUBLANES
                win_sum = win_sum + ub_buf[off:off + m, c0:c1]
            count = jnp.minimum(t_idx + 1, w).astype(F32)
            diff = win_sum / count - cur
            parts.append(_dot(diff.astype(BF16), wpool_ref[gi]))
        ub_buf[0:hist, :] = ub_buf[m:m + hist, :]
        g_b = _dot(h, win_ref[:, 3 * d:4 * d])
        y_b = jnp.concatenate(parts, axis=1) * pscale_ref[...] * _silu(g_b)
        acc = acc + _dot(y_b.astype(BF16), wout_ref[d:2 * d, :])

        _store_rows(o_ref, x + acc, sub)


def _even_layer(x_in, params, casts, batch_major_in):
    d = x_in.shape[-1]
    rows = x_in.size // d
    assert EVEN_TILE_STEPS % PAIR == 0
    m = EVEN_TILE_STEPS * SUBLANES
    pairs = EVEN_TILE_STEPS // PAIR
    mp = pairs * SUBLANES
    hist = max(POOL_WINDOWS) * SUBLANES
    nc = d // S5_CHUNK
    sp = STATES_PER_CHUNK
    block = EVEN_SUB_TILES * m
    n_steps = rows // block
    assert len(params) == EVEN_INPUTS - 1
    cast_in, cast_out, cast_shapes = _cast_specs(casts, n_steps)
    in_specs = [_row_spec(block, d, batch_major_in)] + [_resident_spec(q) for q in params] + cast_in
    return pl.pallas_call(
        functools.partial(_even_kernel, n_cast=len(casts), sub_tiles=EVEN_SUB_TILES),
        grid=(n_steps,),
        in_specs=in_specs,
        out_specs=[_row_spec(block, d, False)] + cast_out,
        out_shape=[jax.ShapeDtypeStruct((rows, d), F32)] + cast_shapes,
        scratch_shapes=[
            pltpu.VMEM((nc, SUBLANES, sp), F32),
            pltpu.VMEM((nc, SUBLANES, sp), F32),
            pltpu.VMEM((pairs, PAIR * SUBLANES, d), F32),
            pltpu.VMEM((2, mp, 2 * sp), F32),
            pltpu.VMEM((2, mp, 2 * sp), F32),
            pltpu.VMEM((pairs, PAIR * SUBLANES, d), F32),
            pltpu.VMEM((hist + m, d), F32),
        ],
        compiler_params=pltpu.CompilerParams(
            dimension_semantics=("arbitrary",),
            vmem_limit_bytes=VMEM_LIMIT_BYTES),
        name="even_layer",
    )(x_in, *[q[0] for q in params], *[q[0] for q in casts])


ODD_INPUTS = 7


def _odd_kernel(*refs, n_cast, final_norm, sub_tiles):
    x_ref, g_ref, win_ref, cw_ref, cb_ref, wout_ref, fg_ref = refs[:ODD_INPUTS]
    cast_src = refs[ODD_INPUTS:ODD_INPUTS + n_cast]
    o_ref = refs[ODD_INPUTS + n_cast]
    cast_dst = refs[ODD_INPUTS + n_cast + 1:ODD_INPUTS + 2 * n_cast + 1]
    (v_buf,) = refs[ODD_INPUTS + 2 * n_cast + 1:]
    _cast_blocks(cast_src, cast_dst)
    i = pl.program_id(0)
    dc, d = wout_ref.shape
    n_chunks = dc // CONV_CHUNK
    hist = (CONV_WIDTH - 1) * SUBLANES
    m = v_buf.shape[1] - hist

    @pl.when(i == 0)
    def _():
        v_buf[:, 0:hist, :] = jnp.zeros((n_chunks, hist, CONV_CHUNK), F32)

    for sub in range(sub_tiles):
        x = _load_rows(x_ref, sub, m)
        h = _rmsnorm(x, g_ref[...]).astype(BF16)
        acc = jnp.zeros((m, d), F32)
        for c in range(n_chunks):
            c0, c1 = c * CONV_CHUNK, (c + 1) * CONV_CHUNK
            x_in = _dot(h, win_ref[:, c0:c1])
            c_gate = _dot(h, win_ref[:, 2 * dc + c0:2 * dc + c1])
            v = c_gate * x_in
            v_buf[c, hist:hist + m, :] = v
            conv = cb_ref[:, c0:c1] + cw_ref[CONV_WIDTH - 1:CONV_WIDTH, c0:c1] * v
            for k in range(1, CONV_WIDTH):
                off = hist - k * SUBLANES
                tap = CONV_WIDTH - 1 - k
                conv = conv + cw_ref[tap:tap + 1, c0:c1] * v_buf[c, off:off + m, :]
            v_buf[c, 0:hist, :] = v_buf[c, m:m + hist, :]
            b_gate = _dot(h, win_ref[:, dc + c0:dc + c1])
            gate = _dot(h, win_ref[:, 3 * dc + c0:3 * dc + c1])
            y = b_gate * conv * _silu(gate)
            acc = acc + _dot(y.astype(BF16), wout_ref[c0:c1, :])
        out = x + acc
        if final_norm:
            out = _rmsnorm(out, fg_ref[...])
        _store_rows(o_ref, out, sub)


def _odd_layer(xt, params, casts, dc, final_norm, out_dtype):
    rows, d = xt.shape
    m = ODD_TILE_STEPS * SUBLANES
    hist = (CONV_WIDTH - 1) * SUBLANES
    block = ODD_SUB_TILES * m
    n_steps = rows // block
    assert len(params) == ODD_INPUTS - 1
    cast_in, cast_out, cast_shapes = _cast_specs(casts, n_steps)
    in_specs = [_row_spec(block, d, False)] + [_resident_spec(q) for q in params] + cast_in
    if final_norm:
        out_shape = jax.ShapeDtypeStruct((SUBLANES, rows // SUBLANES, d), out_dtype)
    else:
        out_shape = jax.ShapeDtypeStruct((rows, d), F32)
    return pl.pallas_call(
        functools.partial(_odd_kernel, n_cast=len(casts), final_norm=final_norm,
                          sub_tiles=ODD_SUB_TILES),
        grid=(n_steps,),
        in_specs=in_specs,
        out_specs=[_row_spec(block, d, final_norm)] + cast_out,
        out_shape=[out_shape] + cast_shapes,
        scratch_shapes=[
            pltpu.VMEM((dc // CONV_CHUNK, hist + m, CONV_CHUNK), F32),
        ],
        compiler_params=pltpu.CompilerParams(
            dimension_semantics=("arbitrary",),
            vmem_limit_bytes=VMEM_LIMIT_BYTES),
        name="odd_layer_final" if final_norm else "odd_layer",
    )(xt, *[q[0] for q in params], *[q[0] for q in casts])


def kernel(x, norm_g, final_g, ev_w_in, ev_w_out, s5_a_re, s5_a_im, s5_log_dt, s5_b_re, s5_b_im,
           s5_c_re, s5_c_im, s5_d, s5_w_glu, s5_b_glu, pool_w, pool_scale, sc_w_in, sc_conv_w,
           sc_conv_b, sc_w_out):
    bsz, seq, d = x.shape
    assert bsz == SUBLANES, "time-major tiling assumes one timestep per 8-row tile"
    depth = norm_g.shape[0]
    assert depth % 2 == 0, "the final norm is fused into the last (odd) layer"

    lam_re, lam_im, b_mat, c_mat, k_mat = _s5_prep(
        s5_a_re, s5_a_im, s5_log_dt, s5_b_re, s5_b_im, s5_c_re, s5_c_im)
    norm_g3 = norm_g[:, None, :]
    final_g3 = final_g[None, None, :]
    pool_shape = pool_w.shape[1:]
    mats_f32 = {
        "ev_w_in": ev_w_in, "s5_w_glu": s5_w_glu, "ev_w_out": ev_w_out,
        "pool_w": pool_w.reshape(pool_w.shape[0], -1, pool_w.shape[-1]),
        "sc_w_in": sc_w_in, "sc_w_out": sc_w_out,
    }
    even_mats = ("ev_w_in", "s5_w_glu", "pool_w", "ev_w_out")
    odd_mats = ("sc_w_in", "sc_w_out")
    mats = {k: (mats_f32[k][0].astype(BF16), None) for k in even_mats}

    act = x.astype(F32)
    for layer in range(depth):
        idx = layer // 2
        nxt = layer + 1
        nxt_names = () if nxt == depth else (even_mats if nxt % 2 == 0 else odd_mats)
        casts = [(mats_f32[k], nxt // 2) for k in nxt_names]
        if layer % 2 == 0:
            w_pool = (mats["pool_w"][0].reshape(pool_shape), None)
            params = [(norm_g3, layer), mats["ev_w_in"], (b_mat, idx), (lam_re, idx),
                      (lam_im, idx), (c_mat, idx), (k_mat, idx), (s5_d[:, None, :], idx),
                      mats["s5_w_glu"], (s5_b_glu[:, None, :], idx), w_pool,
                      (pool_scale[:, None, :], idx), mats["ev_w_out"]]
            act, *cast_out = _even_layer(act, params, casts, batch_major_in=(layer == 0))
        else:
            params = [(norm_g3, layer), mats["sc_w_in"], (sc_conv_w, idx),
                      (sc_conv_b[:, None, :], idx), mats["sc_w_out"], (final_g3, 0)]
            act, *cast_out = _odd_layer(act, params, casts, dc=sc_w_out.shape[1],
                                        final_norm=(nxt == depth), out_dtype=x.dtype)
        mats = {k: (w, None) for k, w in zip(nxt_names, cast_out)}
    return act
```
